```python
import jax, jax.numpy as jnp
from jax import lax
import numpy as np

D_MODEL = 2048
BATCH = 4
SEQ = 8192
DEPTH = 4

GROUP_WIDTH = 128
HEAD_DIM = 128
D_GMLP = D_MODEL // 4
D_SB = D_MODEL // 4
D_MIX = D_GMLP + D_SB
N_GMLP_GROUPS = D_GMLP // GROUP_WIDTH
N_SB_HEADS = D_SB // HEAD_DIM
N_MIX_GROUPS = D_MIX // GROUP_WIDTH
D_IN = 2 * D_GMLP + 3 * D_SB
SPLIT_POINTS = (D_GMLP, 2 * D_GMLP, 2 * D_GMLP + D_SB, 2 * D_GMLP + 2 * D_SB)
CHUNK = 128
Q_BLOCK = 128
N_GROUPS = 4
EXPERTS_PER_GROUP = 8
N_EXPERTS = N_GROUPS * EXPERTS_PER_GROUP
TOP_K = 2
D_EXPERT = D_MODEL // 8
EXPERT_BLOCK = 128
N_MOD = 6
EPS = 1e-6

kernel_name = "hybrid_sgmlp_stickbreak_hmoe_adaln"


def rms_norm(x, g):
    xf = x.astype(jnp.float32)
    y = xf * lax.rsqrt(jnp.mean(xf * xf, axis=-1, keepdims=True) + EPS)
    return (y * g.astype(jnp.float32)).astype(x.dtype)


def chunked_spatial_gating(u, v, norm_g, ws, bs):
    B, S, _ = u.shape
    u = jax.nn.gelu(u, approximate=False)
    v = jax.nn.gelu(v, approximate=False).reshape(B, S // CHUNK, CHUNK, N_GMLP_GROUPS, GROUP_WIDTH)
    v = rms_norm(v, norm_g.reshape(N_GMLP_GROUPS, GROUP_WIDTH))
    w = ws * jnp.tril(jnp.ones((CHUNK, CHUNK), ws.dtype))
    s = jnp.einsum('gtp,bcpgk->bctgk', w, v) + bs.T[:, :, None]
    return u * s.reshape(B, S, D_GMLP)


def stick_breaking_attention(q, k, v):
    B, S, H, dh = q.shape
    nq = S // Q_BLOCK
    scale = HEAD_DIM ** -0.5
    qf = q.astype(jnp.float32).transpose(0, 2, 1, 3)
    kf = k.astype(jnp.float32).transpose(0, 2, 1, 3)
    vt = v.transpose(0, 2, 1, 3)
    idx = jnp.arange(Q_BLOCK, dtype=jnp.int32)
    after_in_block = (idx[:, None] > idx[None, :]).astype(jnp.float32)
    outs = []
    for n in range(nq):
        nk = n + 1
        kl = nk * Q_BLOCK
        qb = qf[:, :, n * Q_BLOCK:kl]
        z = jnp.einsum('bhqd,bhkd->bhqk', qb, kf[:, :, :kl]) * scale
        q_pos = n * Q_BLOCK + idx
        causal = jnp.arange(kl, dtype=jnp.int32)[None, :] < q_pos[:, None]
        log_fail = jnp.where(causal, jax.nn.log_sigmoid(-z), 0.0).reshape(B, H, Q_BLOCK, nk, Q_BLOCK)
        within = jnp.einsum('bhqni,ij->bhqnj', log_fail, after_in_block)
        blk = jnp.sum(log_fail, axis=-1)
        later = lax.cumsum(blk, axis=3, reverse=True) - blk
        log_a = jax.nn.log_sigmoid(z) + (within + later[..., None]).reshape(B, H, Q_BLOCK, kl)
        a = jnp.where(causal, jnp.exp(log_a), 0.0)
        outs.append(jnp.einsum('bhqk,bhkd->bhqd', a.astype(vt.dtype), vt[:, :, :kl]))
    o = jnp.concatenate(outs, axis=2)
    return o.transpose(0, 2, 1, 3).reshape(B, S, H * dh)


def hierarchical_moe(h, w_group, b_group, w_route, b_route, w_gate, w_up, w_down):
    n, d = h.shape
    hf = h.astype(jnp.float32)
    g_logits = hf @ w_group.astype(jnp.float32) + b_group.astype(jnp.float32)
    g_prob = jax.nn.softmax(g_logits, axis=-1)
    g_idx = jnp.argmax(g_logits, axis=-1).astype(jnp.int32)
    p_group = jnp.take_along_axis(g_prob, g_idx[:, None], axis=-1)
    e_logits = (hf @ w_route.astype(jnp.float32) + b_route.astype(jnp.float32)).reshape(n, N_GROUPS, EXPERTS_PER_GROUP)
    e_logits = jnp.take_along_axis(e_logits, g_idx[:, None, None], axis=1)[:, 0]
    top_val, top_idx = lax.top_k(e_logits, TOP_K)
    weights = jax.nn.softmax(top_val, axis=-1) * p_group
    expert_id = g_idx[:, None] * EXPERTS_PER_GROUP + top_idx.astype(jnp.int32)

    a = n * TOP_K
    flat_e = expert_id.reshape(a)
    flat_tok = jnp.repeat(jnp.arange(n, dtype=jnp.int32), TOP_K)
    order = jnp.argsort(flat_e)
    sorted_e = flat_e[order]
    counts = jnp.bincount(flat_e, length=N_EXPERTS).astype(jnp.int32)
    start = jnp.cumsum(counts) - counts
    padded = (counts + EXPERT_BLOCK - 1) // EXPERT_BLOCK * EXPERT_BLOCK
    pend = jnp.cumsum(padded)
    pstart = pend - padded
    rank = jnp.arange(a, dtype=jnp.int32) - start[sorted_e]
    slot = jnp.zeros((a,), jnp.int32).at[order].set((pstart[sorted_e] + rank).astype(jnp.int32))
    n_blocks = (a + EXPERT_BLOCK - 1) // EXPERT_BLOCK + N_EXPERTS
    capacity = n_blocks * EXPERT_BLOCK
    slot_tok = jnp.full((capacity,), n, jnp.int32).at[slot].set(flat_tok)
    block_e = jnp.minimum(
        jnp.searchsorted(pend, jnp.arange(n_blocks, dtype=jnp.int32) * EXPERT_BLOCK, side='right'),
        N_EXPERTS - 1).astype(jnp.int32)
    h_pad = jnp.concatenate([h, jnp.zeros((1, d), h.dtype)], axis=0)
    xs = h_pad[slot_tok].reshape(n_blocks, EXPERT_BLOCK, d)

    def expert_block(args):
        xb, e = args
        act = jax.nn.silu(xb @ w_gate[e]) * (xb @ w_up[e])
        return act @ w_down[e]

    ys = lax.map(expert_block, (xs, block_e)).reshape(capacity, d)
    picked = ys[slot].reshape(n, TOP_K, d)
    return jnp.einsum('nkd,nk->nd', picked, weights.astype(picked.dtype))


def setup_inputs(seed: int = 0) -> dict:
    key = jax.random.key(seed)
    ks = jax.random.split(key, 22)
    f32 = jnp.float32
    L, D = DEPTH, D_MODEL

    def nrm(k, shape, s):
        return jax.random.normal(k, shape, f32) * s

    return {
        "x": nrm(ks[0], (BATCH, SEQ, D), 1.0),
        "c": nrm(ks[1], (BATCH, D), 1.0),
        "w_mod": nrm(ks[2], (D, N_MOD * D), 0.5 * D ** -0.5),
        "b_mod": nrm(ks[3], (N_MOD * D,), 0.02),
        "mod_layer": nrm(ks[4], (L, N_MOD * D), 0.1),
        "norm1_g": 1.0 + nrm(ks[5], (L, D), 0.02),
        "w_in": nrm(ks[6], (L, D, D_IN), D ** -0.5),
        "gm_norm_g": 1.0 + nrm(ks[7], (L, D_GMLP), 0.02),
        "gm_ws": nrm(ks[8], (L, N_GMLP_GROUPS, CHUNK, CHUNK), CHUNK ** -0.5),
        "gm_bs": 1.0 + nrm(ks[9], (L, N_GMLP_GROUPS, CHUNK), 0.1),
        "q_norm_g": 1.0 + nrm(ks[10], (L, HEAD_DIM), 0.02),
        "k_norm_g": 1.0 + nrm(ks[11], (L, HEAD_DIM), 0.02),
        "out_norm_g": 1.0 + nrm(ks[12], (L, D_MIX), 0.02),
        "w_out": nrm(ks[13], (L, D_MIX, D), D_MIX ** -0.5),
        "norm2_g": 1.0 + nrm(ks[14], (L, D), 0.02),
        "w_group": nrm(ks[15], (L, D, N_GROUPS), D ** -0.5),
        "b_group": nrm(ks[16], (L, N_GROUPS), 0.01),
        "w_route": nrm(ks[17], (L, D, N_EXPERTS), D ** -0.5),
        "b_route": nrm(ks[18], (L, N_EXPERTS), 0.01),
        "w_gate": nrm(ks[19], (L, N_EXPERTS, D, D_EXPERT), D ** -0.5),
        "w_up": nrm(ks[20], (L, N_EXPERTS, D, D_EXPERT), D ** -0.5),
        "w_down": nrm(ks[21], (L, N_EXPERTS, D_EXPERT, D), D_EXPERT ** -0.5),
    }


def reference(x, c, w_mod, b_mod, mod_layer, norm1_g, w_in, gm_norm_g, gm_ws, gm_bs,
              q_norm_g, k_norm_g, out_norm_g, w_out, norm2_g, w_group, b_group,
              w_route, b_route, w_gate, w_up, w_down):
    B, S, D = x.shape
    mod_shared = jax.nn.silu(c) @ w_mod + b_mod
    for l in range(DEPTH):
        mod = (mod_shared + mod_layer[l])[:, None, :]
        shift1, scale1, gate1, shift2, scale2, gate2 = jnp.split(mod, N_MOD, axis=-1)

        h = rms_norm(x, norm1_g[l]) * (1.0 + scale1) + shift1
        proj = h @ w_in[l]
        u_a, v_a, q, k, v = jnp.split(proj, SPLIT_POINTS, axis=-1)
        y_a = chunked_spatial_gating(u_a, v_a, gm_norm_g[l], gm_ws[l], gm_bs[l])
        q = rms_norm(q.reshape(B, S, N_SB_HEADS, HEAD_DIM), q_norm_g[l])
        k = rms_norm(k.reshape(B, S, N_SB_HEADS, HEAD_DIM), k_norm_g[l])
        v = v.reshape(B, S, N_SB_HEADS, HEAD_DIM)
        y_b = stick_breaking_attention(q, k, v)
        y = jnp.concatenate([y_a, y_b], axis=-1).reshape(B, S, N_MIX_GROUPS, GROUP_WIDTH)
        y = rms_norm(y, out_norm_g[l].reshape(N_MIX_GROUPS, GROUP_WIDTH)).reshape(B, S, D_MIX)
        x = x + gate1 * (y @ w_out[l])

        h = rms_norm(x, norm2_g[l]) * (1.0 + scale2) + shift2
        m = hierarchical_moe(h.reshape(B * S, D), w_group[l], b_group[l], w_route[l], b_route[l],
                             w_gate[l], w_up[l], w_down[l]).reshape(B, S, D)
        x = x + gate2 * m
    return x
```

```python
import functools

import jax
import jax.numpy as jnp
from jax import lax
from jax.experimental import pallas as pl
from jax.experimental.pallas import tpu as pltpu

F32 = jnp.float32
BF16 = jnp.bfloat16

EPS = 1e-6
LANES = 128
N_MOD = 6
N_GROUPS = 4
EXPERTS_PER_GROUP = 8
N_EXPERTS = N_GROUPS * EXPERTS_PER_GROUP
TOP_K = 2
LOG_WEIGHT_FLOOR = -104.0
ROW_GROUP = 8
VMEM_LIMIT = 56 * 1024 * 1024


def _rms(x):
    return x * lax.rsqrt(jnp.mean(x * x, axis=-1, keepdims=True) + EPS)


def _gelu(x):
    return 0.5 * x * (1.0 + lax.erf(x * (2.0 ** -0.5)))


def _params(*sem):
    return pltpu.CompilerParams(dimension_semantics=sem, vmem_limit_bytes=VMEM_LIMIT)


def _mod_body(c_ref, w_ref, b_ref, ml_ref, o_ref):
    c = c_ref[...]
    sc = c * jax.nn.sigmoid(c)
    r = jnp.dot(sc, w_ref[...], preferred_element_type=F32,
                precision=lax.Precision.HIGHEST) + b_ref[...]
    for l in range(o_ref.shape[0]):
        o_ref[l] = r + ml_ref[l:l + 1, :]


def _modulation(c, w_mod, b_mod, mod_layer):
    b, d = c.shape
    depth, w = mod_layer.shape
    tn = 1024
    return pl.pallas_call(
        _mod_body,
        out_shape=jax.ShapeDtypeStruct((depth, b, w), F32),
        grid=(w // tn,),
        in_specs=[pl.BlockSpec((b, d), lambda j: (0, 0)),
                  pl.BlockSpec((d, tn), lambda j: (0, j)),
                  pl.BlockSpec((1, tn), lambda j: (0, j)),
                  pl.BlockSpec((depth, tn), lambda j: (0, j))],
        out_specs=pl.BlockSpec((depth, b, tn), lambda j: (0, 0, j)),
        compiler_params=_params("arbitrary"),
        name="modulation",
    )(c, w_mod, b_mod.reshape(1, w), mod_layer)


def _mix_in_body(has_moe, *refs):
    refs = list(refs)
    x_ref = refs.pop(0)
    if has_moe:
        m0_ref, m1_ref, g2_ref = refs.pop(0), refs.pop(0), refs.pop(0)
    (mod_ref, g1_ref, win_ref, gmg_ref, ws_ref, bs_ref, qg_ref, kg_ref, og_ref) = refs[:9]
    refs = refs[9:]
    if has_moe:
        xo_ref = refs.pop(0)
    ya_ref, q_ref, k_ref, v_ref = refs

    x = x_ref[...]
    if has_moe:
        x = x + g2_ref[...] * (m0_ref[...] + m1_ref[...])
        xo_ref[...] = x
    tm = x.shape[0]
    dq = ya_ref.shape[1]
    shift1, scale1 = mod_ref[0], mod_ref[1]
    h = _rms(x) * g1_ref[...] * (1.0 + scale1) + shift1
    proj = jnp.dot(h.astype(BF16), win_ref[...], preferred_element_type=F32)

    row = lax.broadcasted_iota(jnp.int32, (LANES, LANES), 0)
    col = lax.broadcasted_iota(jnp.int32, (LANES, LANES), 1)
    causal = row >= col
    for g in range(dq // LANES):
        sl = slice(g * LANES, (g + 1) * LANES)
        u = _gelu(proj[:, g * LANES:(g + 1) * LANES])
        vg = _gelu(proj[:, dq + g * LANES:dq + (g + 1) * LANES])
        vg = (_rms(vg) * gmg_ref[:, sl]).astype(BF16)
        wg = jnp.where(causal, ws_ref[g], 0.0).astype(BF16)
        for c in range(tm // LANES):
            rs = slice(c * LANES, (c + 1) * LANES)
            s = jnp.dot(wg, vg[rs], preferred_element_type=F32) + bs_ref[g]
            ya = u[rs] * s
            ya_ref[rs, sl] = (_rms(ya) * og_ref[:, sl]).astype(BF16)
    for hd in range(dq // LANES):
        sl = slice(hd * LANES, (hd + 1) * LANES)
        qh = proj[:, 2 * dq + hd * LANES:2 * dq + (hd + 1) * LANES]
        kh = proj[:, 3 * dq + hd * LANES:3 * dq + (hd + 1) * LANES]
        q_ref[:, sl] = (_rms(qh) * qg_ref[...]).astype(BF16)
        k_ref[:, sl] = (_rms(kh) * kg_ref[...]).astype(BF16)
    v_ref[...] = proj[:, 4 * dq:].astype(BF16)


def _mix_in(x, moe, mod, g1, w_in, gm_g, gm_ws, gm_bs, q_g, k_g, out_g, *, seq, tm):
    n, d = x.shape
    dq = w_in.shape[1] // 5
    ng = dq // LANES
    per_b = seq // tm
    row_spec = pl.BlockSpec((tm, d), lambda i: (i, 0))
    const2 = lambda i: (0, 0)
    in_specs = [row_spec]
    args = [x]
    if moe is not None:
        m, gate2 = moe
        in_specs += [row_spec,
                     pl.BlockSpec((tm, d), lambda i: (n // tm + i, 0)),
                     pl.BlockSpec((None, 1, d), lambda i: (i // per_b, 0, 0))]
        args += [m, m, gate2]
    in_specs += [pl.BlockSpec((None, N_MOD, 1, d), lambda i: (i // per_b, 0, 0, 0)),
                 pl.BlockSpec((1, d), const2),
                 pl.BlockSpec(w_in.shape, const2),
                 pl.BlockSpec((1, dq), const2),
                 pl.BlockSpec((ng, LANES, LANES), lambda i: (0, 0, 0)),
                 pl.BlockSpec((ng, LANES, 1), lambda i: (0, 0, 0)),
                 pl.BlockSpec((1, LANES), const2),
                 pl.BlockSpec((1, LANES), const2),
                 pl.BlockSpec((1, dq), const2)]
    args += [mod, g1.reshape(1, d), w_in, gm_g.reshape(1, dq), gm_ws, gm_bs.reshape(ng, LANES, 1),
             q_g.reshape(1, LANES), k_g.reshape(1, LANES), out_g.reshape(1, -1)]
    act = jax.ShapeDtypeStruct((n, dq), BF16)
    act_spec = pl.BlockSpec((tm, dq), lambda i: (i, 0))
    out_shape = [act, act, act, act]
    out_specs = [act_spec, act_spec, act_spec, act_spec]
    if moe is not None:
        out_shape = [jax.ShapeDtypeStruct((n, d), F32)] + out_shape
        out_specs = [row_spec] + out_specs
    outs = pl.pallas_call(
        functools.partial(_mix_in_body, moe is not None),
        out_shape=out_shape, grid=(n // tm,), in_specs=in_specs, out_specs=out_specs,
        compiler_params=_params("arbitrary"), name="mix_in",
    )(*args)
    if moe is None:
        return (x,) + tuple(outs)
    return tuple(outs)


def _stick_break_body(q_ref, k_ref, v_ref, og_ref, o_ref, carry_ref, acc_ref):
    tq = q_ref.shape[0]
    nsub = tq // LANES
    qi = pl.program_id(2)
    scale = LANES ** -0.5
    r2 = lax.broadcasted_iota(jnp.int32, (LANES, 2 * LANES), 0)
    c2 = lax.broadcasted_iota(jnp.int32, (LANES, 2 * LANES), 1)
    tri_ones = jnp.where((r2 > c2) | (c2 >= LANES), 1.0, 0.0).astype(BF16)

    carry_ref[...] = jnp.zeros_like(carry_ref)
    acc_ref[...] = jnp.zeros_like(acc_ref)

    def key_block(r0, kb, masked):
        rows = tq - r0
        q = q_ref[r0:, :]
        start = pl.multiple_of(kb * LANES, LANES)
        kblk = k_ref[pl.ds(start, LANES), :]
        vblk = v_ref[pl.ds(start, LANES), :]
        z = lax.dot_general(q, kblk, (((1,), (1,)), ((), ())),
                            preferred_element_type=F32) * scale
        lf = -(jnp.maximum(z, 0.0) + jnp.log1p(jnp.exp(-jnp.abs(z))))
        if masked:
            rr = lax.broadcasted_iota(jnp.int32, (rows, LANES), 0)
            cc = lax.broadcasted_iota(jnp.int32, (rows, LANES), 1)
            keep = cc < rr
            lf = jnp.where(keep, lf, 0.0)
        wb = jnp.dot(lf.astype(BF16), tri_ones, preferred_element_type=F32)
        carry = carry_ref[r0:, :]
        a = jnp.exp(lf + z + wb[:, :LANES] + carry)
        if masked:
            a = jnp.where(keep, a, 0.0)
        acc_ref[r0:, :] += jnp.dot(a.astype(BF16), vblk, preferred_element_type=F32)
        carry_ref[r0:, :] = carry + wb[:, LANES:]

    for j in reversed(range(nsub)):
        key_block(j * LANES, qi * nsub + j, True)

    def more(state):
        kb, live = state
        return jnp.logical_and(kb >= 0, live)

    def step(state):
        kb, _ = state
        key_block(0, kb, False)
        return kb - 1, jnp.max(carry_ref[...]) > LOG_WEIGHT_FLOOR

    lax.while_loop(more, step, (qi * nsub - 1, jnp.max(carry_ref[...]) > LOG_WEIGHT_FLOOR))
    o_ref[...] = (_rms(acc_ref[...]) * og_ref[...]).astype(BF16)


def _stick_break(q, k, v, out_g, *, batch, seq, tq):
    n, dq = q.shape
    nh = dq // LANES
    per_b = seq // tq
    q_spec = pl.BlockSpec((tq, LANES), lambda b, h, i: (b * per_b + i, h))
    kv_spec = pl.BlockSpec((seq, LANES), lambda b, h, i: (b, h))
    n_gm = out_g.shape[0] // LANES - nh
    return pl.pallas_call(
        _stick_break_body,
        out_shape=jax.ShapeDtypeStruct((n, dq), BF16),
        grid=(batch, nh, per_b),
        in_specs=[q_spec, kv_spec, kv_spec,
                  pl.BlockSpec((1, LANES), lambda b, h, i: (0, n_gm + h))],
        out_specs=q_spec,
        scratch_shapes=[pltpu.VMEM((tq, LANES), F32), pltpu.VMEM((tq, LANES), F32)],
        compiler_params=_params("arbitrary", "arbitrary", "arbitrary"),
        name="stick_break",
    )(q, k, v, out_g.reshape(1, -1))


def _mix_out_body(x_ref, ya_ref, yb_ref, mod_ref, woa_ref, wob_ref, g2_ref, wr_ref, br_ref,
                  x1_ref, h2_ref, ri_ref, rw_ref, cnt_ref, count_scr):
    tm = x_ref.shape[0]

    @pl.when(pl.program_id(0) == 0)
    def _():
        count_scr[...] = jnp.zeros_like(count_scr)

    gate1, shift2, scale2 = mod_ref[2], mod_ref[3], mod_ref[4]
    y = (jnp.dot(ya_ref[...], woa_ref[...], preferred_element_type=F32)
         + jnp.dot(yb_ref[...], wob_ref[...], preferred_element_type=F32))
    x1 = x_ref[...] + gate1 * y
    x1_ref[...] = x1
    h2 = _rms(x1) * g2_ref[...] * (1.0 + scale2) + shift2
    h2_ref[...] = h2

    lg = jnp.dot(h2, wr_ref[...], preferred_element_type=F32,
                 precision=lax.Precision.HIGHEST) + br_ref[...]
    lane = lax.broadcasted_iota(jnp.int32, (tm, LANES), 1)
    lane_f = lane.astype(F32)
    neg = -jnp.inf

    def first_max(vals):
        m = jnp.max(vals, axis=-1, keepdims=True)
        idx = jnp.min(jnp.where(vals == m, lane_f, float(LANES)), axis=-1, keepdims=True)
        return m, idx.astype(jnp.int32)

    is_group = lane < N_GROUPS
    gmax, gidx = first_max(jnp.where(is_group, lg, neg))
    p_group = 1.0 / jnp.sum(jnp.where(is_group, jnp.exp(lg - gmax), 0.0), axis=-1, keepdims=True)
    lo = N_GROUPS + gidx * EXPERTS_PER_GROUP
    el = jnp.where((lane >= lo) & (lane < lo + EXPERTS_PER_GROUP), lg, neg)
    m1, i1 = first_max(el)
    m2, i2 = first_max(jnp.where(lane == i1, neg, el))
    t = jnp.exp(m2 - m1)
    w1 = p_group / (1.0 + t)
    w2 = w1 * t
    e1 = i1 - N_GROUPS
    e2 = i2 - N_GROUPS

    hit1 = lane == e1
    hit2 = lane == e2
    onehot = jnp.where(hit1 | hit2, 1.0, 0.0)
    rr = lax.broadcasted_iota(jnp.int32, (tm, tm), 0)
    cc = lax.broadcasted_iota(jnp.int32, (tm, tm), 1)
    earlier = jnp.where(rr > cc, 1.0, 0.0).astype(BF16)
    base = jnp.dot(earlier, onehot.astype(BF16), preferred_element_type=F32) + count_scr[...]
    r1 = jnp.sum(jnp.where(hit1, base, 0.0), axis=-1, keepdims=True).astype(jnp.int32)
    r2 = jnp.sum(jnp.where(hit2, base, 0.0), axis=-1, keepdims=True).astype(jnp.int32)
    count_scr[...] += jnp.sum(onehot, axis=0, keepdims=True)
    cnt_ref[...] = count_scr[...].astype(jnp.int32)

    l8 = lax.broadcasted_iota(jnp.int32, ri_ref.shape, 1)
    ri_ref[...] = jnp.where(l8 == 0, e1, jnp.where(l8 == 1, e2, jnp.where(l8 == 2, r1, r2)))
    rw_ref[...] = jnp.where(l8 == 0, w1, w2)


def _mix_out(x, ya, yb, mod, w_out, g2, w_router, b_router, *, seq, tm):
    n, d = x.shape
    dq = ya.shape[1]
    per_b = seq // tm
    row_spec = pl.BlockSpec((tm, d), lambda i: (i, 0))
    act_spec = pl.BlockSpec((tm, dq), lambda i: (i, 0))
    small_spec = pl.BlockSpec((tm, 8), lambda i: (i, 0))
    const2 = lambda i: (0, 0)
    return pl.pallas_call(
        _mix_out_body,
        out_shape=[jax.ShapeDtypeStruct((n, d), F32), jax.ShapeDtypeStruct((n, d), F32),
                   jax.ShapeDtypeStruct((n, 8), jnp.int32), jax.ShapeDtypeStruct((n, 8), F32),
                   jax.ShapeDtypeStruct((1, LANES), jnp.int32)],
        grid=(n // tm,),
        in_specs=[row_spec, act_spec, act_spec,
                  pl.BlockSpec((None, N_MOD, 1, d), lambda i: (i // per_b, 0, 0, 0)),
                  pl.BlockSpec((dq, d), const2),
                  pl.BlockSpec((dq, d), lambda i: (1, 0)),
                  pl.BlockSpec((1, d), const2),
                  pl.BlockSpec((d, LANES), const2),
                  pl.BlockSpec((1, LANES), const2)],
        out_specs=[row_spec, row_spec, small_spec, small_spec,
                   pl.BlockSpec((1, LANES), const2)],
        scratch_shapes=[pltpu.VMEM((1, LANES), F32)],
        compiler_params=_params("arbitrary"),
        name="mix_out",
    )(x, ya, yb, mod, w_out, w_out, g2.reshape(1, d), w_router, b_router)


def _experts_body(n_tok, code_ref, be_ref, nv_ref, nu_ref, h_hbm, w_ref, wg_ref, wu_ref, wd_ref,
                  out_hbm, xbuf, ybuf, gsem, ssem):
    te = xbuf.shape[0]
    i = pl.program_id(0)

    @pl.when(i < nu_ref[0])
    def _():
        base = i * te
        n_valid = nv_ref[i]

        def gather(r, carry):
            tok = lax.rem(code_ref[base + r], n_tok)
            pltpu.make_async_copy(h_hbm.at[pl.ds(tok, 1), :], xbuf.at[pl.ds(r, 1), :], gsem).start()
            return carry

        lax.fori_loop(0, te, gather, 0, unroll=8)
        pltpu.make_async_copy(h_hbm.at[pl.ds(0, te), :], xbuf, gsem).wait()

        xb = xbuf[...].astype(BF16)
        g = jnp.dot(xb, wg_ref[...], preferred_element_type=F32)
        u = jnp.dot(xb, wu_ref[...], preferred_element_type=F32)
        act = (g * jax.nn.sigmoid(g) * u).astype(BF16)
        ybuf[...] = jnp.dot(act, wd_ref[...], preferred_element_type=F32) * w_ref[...]

        def scatter(r):
            dst = code_ref[base + r]
            pltpu.make_async_copy(ybuf.at[pl.ds(r, 1), :], out_hbm.at[pl.ds(dst, 1), :], ssem).start()

        def wait_rows(rows):
            pltpu.make_async_copy(ybuf.at[pl.ds(0, rows), :], out_hbm.at[pl.ds(0, rows), :],
                                  ssem).wait()

        def scatter_group(j, carry):
            for u in range(ROW_GROUP):
                scatter(j * ROW_GROUP + u)
            return carry

        def scatter_one(r, carry):
            scatter(r)
            return carry

        def wait_group(j, carry):
            wait_rows(ROW_GROUP)
            return carry

        def wait_one(j, carry):
            wait_rows(1)
            return carry

        n_groups = n_valid // ROW_GROUP
        lax.fori_loop(0, n_groups, scatter_group, 0)
        lax.fori_loop(n_groups * ROW_GROUP, n_valid, scatter_one, 0)
        lax.fori_loop(0, n_groups, wait_group, 0)
        lax.fori_loop(n_groups * ROW_GROUP, n_valid, wait_one, 0)


def _experts(h2, code, block_e, n_valid, n_used, slot_w, w_gate, w_up, w_down, *, te):
    n, d = h2.shape
    de = w_gate.shape[2]
    n_blocks = block_e.shape[0]
    grid_spec = pltpu.PrefetchScalarGridSpec(
        num_scalar_prefetch=4,
        grid=(n_blocks,),
        in_specs=[pl.BlockSpec(memory_space=pl.ANY),
                  pl.BlockSpec((te, 1), lambda i, code, be, nv, nu: (i, 0)),
                  pl.BlockSpec((None, d, de), lambda i, code, be, nv, nu: (be[i], 0, 0)),
                  pl.BlockSpec((None, d, de), lambda i, code, be, nv, nu: (be[i], 0, 0)),
                  pl.BlockSpec((None, de, d), lambda i, code, be, nv, nu: (be[i], 0, 0))],
        out_specs=pl.BlockSpec(memory_space=pl.ANY),
        scratch_shapes=[pltpu.VMEM((te, d), F32), pltpu.VMEM((te, d), F32),
                        pltpu.SemaphoreType.DMA(()), pltpu.SemaphoreType.DMA(())],
    )
    return pl.pallas_call(
        functools.partial(_experts_body, n),
        out_shape=jax.ShapeDtypeStruct((TOP_K * n, d), F32),
        grid_spec=grid_spec,
        compiler_params=_params("arbitrary"),
        name="moe_experts",
    )(code, block_e, n_valid, n_used, h2, slot_w, w_gate, w_up, w_down)


def _dispatch(ri, rw, counts, *, te):
    n = ri.shape[0]
    counts = counts[0, :N_EXPERTS]
    padded = (counts + te - 1) // te * te
    pend = jnp.cumsum(padded)
    pstart = pend - padded
    n_blocks = TOP_K * n // te + N_EXPERTS
    cap = n_blocks * te
    slot = (pstart[ri[:, :TOP_K]] + ri[:, TOP_K:2 * TOP_K]).reshape(-1)
    tok = jnp.arange(n, dtype=jnp.int32)[:, None]
    code_valid = (jnp.arange(TOP_K, dtype=jnp.int32)[None, :] * n + tok).reshape(-1)
    code = jnp.zeros((cap,), jnp.int32).at[slot].set(code_valid)
    slot_w = jnp.zeros((cap,), F32).at[slot].set(rw[:, :TOP_K].reshape(-1)).reshape(cap, 1)
    block_start = jnp.arange(n_blocks, dtype=jnp.int32) * te
    block_e = jnp.minimum(jnp.searchsorted(pend, block_start, side='right'),
                          N_EXPERTS - 1).astype(jnp.int32)
    n_valid = jnp.clip(pstart[block_e] + counts[block_e] - block_start, 0, te).astype(jnp.int32)
    n_used = (pend[-1:] // te).astype(jnp.int32)
    return code, block_e, n_valid, n_used, slot_w


def _combine_body(x_ref, m0_ref, m1_ref, g2_ref, o_ref):
    o_ref[...] = x_ref[...] + g2_ref[...] * (m0_ref[...] + m1_ref[...])


def _combine(x, m, gate2, *, seq, tm):
    n, d = x.shape
    per_b = seq // tm
    row_spec = pl.BlockSpec((tm, d), lambda i: (i, 0))
    return pl.pallas_call(
        _combine_body,
        out_shape=jax.ShapeDtypeStruct((n, d), F32),
        grid=(n // tm,),
        in_specs=[row_spec, row_spec,
                  pl.BlockSpec((tm, d), lambda i: (n // tm + i, 0)),
                  pl.BlockSpec((None, 1, d), lambda i: (i // per_b, 0, 0))],
        out_specs=row_spec,
        compiler_params=_params("arbitrary"),
        name="moe_combine",
    )(x, m, m, gate2)


def _tiles(seq):
    tm = min(256, seq)
    tq = min(256, seq)
    te = 256
    return tm, tq, te


def kernel(x, c, w_mod, b_mod, mod_layer, norm1_g, w_in, gm_norm_g, gm_ws, gm_bs, q_norm_g, k_norm_g, out_norm_g, w_out, norm2_g, w_group, b_group, w_route, b_route, w_gate, w_up, w_down):
    batch, seq, d = x.shape
    depth = mod_layer.shape[0]
    n = batch * seq
    tm, tq, te = _tiles(seq)
    assert seq % tm == 0 and seq % tq == 0 and tm % LANES == 0 and tq % LANES == 0
    assert (TOP_K * n) % te == 0 and w_group.shape[2] == N_GROUPS and w_route.shape[2] == N_EXPERTS

    mod_all = _modulation(c, w_mod, b_mod, mod_layer).reshape(depth, batch, N_MOD, 1, d)
    w_in_b, w_out_b = w_in.astype(BF16), w_out.astype(BF16)
    w_gate_b, w_up_b, w_down_b = w_gate.astype(BF16), w_up.astype(BF16), w_down.astype(BF16)
    pad = LANES - N_GROUPS - N_EXPERTS
    w_router = jnp.pad(jnp.concatenate([w_group, w_route], axis=2), ((0, 0), (0, 0), (0, pad)))
    b_router = jnp.pad(jnp.concatenate([b_group, b_route], axis=1), ((0, 0), (0, pad)))

    xf = x.reshape(n, d)
    moe = None
    for l in range(depth):
        xf, ya, q, k, v = _mix_in(xf, moe, mod_all[l], norm1_g[l], w_in_b[l], gm_norm_g[l], gm_ws[l],
                                  gm_bs[l], q_norm_g[l], k_norm_g[l], out_norm_g[l], seq=seq, tm=tm)
        yb = _stick_break(q, k, v, out_norm_g[l], batch=batch, seq=seq, tq=tq)
        xf, h2, ri, rw, counts = _mix_out(xf, ya, yb, mod_all[l], w_out_b[l], norm2_g[l],
                                          w_router[l], b_router[l:l + 1], seq=seq, tm=tm)
        code, block_e, n_valid, n_used, slot_w = _dispatch(ri, rw, counts, te=te)
        m = _experts(h2, code, block_e, n_valid, n_used, slot_w,
                     w_gate_b[l], w_up_b[l], w_down_b[l], te=te)
        moe = (m, mod_all[l][:, 5])
    out = _combine(xf, *moe, seq=seq, tm=tm)
    return out.reshape(batch, seq, d)
```

```python
import functools

import jax
import jax.numpy as jnp
from jax import lax
from jax.experimental import pallas as pl
from jax.experimental.pallas import tpu as pltpu

F32 = jnp.float32
BF16 = jnp.bfloat16

EPS = 1e-6
LANES = 128
N_MOD = 6
N_GROUPS = 4
EXPERTS_PER_GROUP = 8
N_EXPERTS = N_GROUPS * EXPERTS_PER_GROUP
TOP_K = 2
LOG_WEIGHT_FLOOR = -104.0
ROW_GROUP = 8
VMEM_LIMIT = 56 * 1024 * 1024


def _rms(x):
    return x * lax.rsqrt(jnp.mean(x * x, axis=-1, keepdims=True) + EPS)


def _gelu(x):
    return 0.5 * x * (1.0 + lax.erf(x * (2.0 ** -0.5)))


def _params(*sem):
    return pltpu.CompilerParams(dimension_semantics=sem, vmem_limit_bytes=VMEM_LIMIT)


def _mod_body(c_ref, w_ref, b_ref, ml_ref, o_ref):
    c = c_ref[...]
    sc = c * jax.nn.sigmoid(c)
    r = jnp.dot(sc, w_ref[...], preferred_element_type=F32,
                precision=lax.Precision.HIGHEST) + b_ref[...]
    for l in range(o_ref.shape[0]):
        o_ref[l] = r + ml_ref[l:l + 1, :]


def _modulation(c, w_mod, b_mod, mod_layer):
    b, d = c.shape
    depth, w = mod_layer.shape
    tn = 1024
    return pl.pallas_call(
        _mod_body,
        out_shape=jax.ShapeDtypeStruct((depth, b, w), F32),
        grid=(w // tn,),
        in_specs=[pl.BlockSpec((b, d), lambda j: (0, 0)),
                  pl.BlockSpec((d, tn), lambda j: (0, j)),
                  pl.BlockSpec((1, tn), lambda j: (0, j)),
                  pl.BlockSpec((depth, tn), lambda j: (0, j))],
        out_specs=pl.BlockSpec((depth, b, tn), lambda j: (0, 0, j)),
        compiler_params=_params("arbitrary"),
        name="modulation",
    )(c, w_mod, b_mod.reshape(1, w), mod_layer)


def _mix_in_body(has_moe, *refs):
    refs = list(refs)
    x_ref = refs.pop(0)
    if has_moe:
        m0_ref, m1_ref, rw_ref, g2_ref = refs.pop(0), refs.pop(0), refs.pop(0), refs.pop(0)
    (mod_ref, g1_ref, win_ref, gmg_ref, ws_ref, bs_ref, qg_ref, kg_ref, og_ref) = refs[:9]
    refs = refs[9:]
    if has_moe:
        xo_ref = refs.pop(0)
    ya_ref, q_ref, k_ref, v_ref = refs

    x = x_ref[...]
    if has_moe:
        x = x + g2_ref[...] * (rw_ref[:, 0:1] * m0_ref[...] + rw_ref[:, 1:2] * m1_ref[...])
        xo_ref[...] = x
    tm = x.shape[0]
    dq = ya_ref.shape[1]
    shift1, scale1 = mod_ref[0], mod_ref[1]
    h = _rms(x) * g1_ref[...] * (1.0 + scale1) + shift1
    proj = jnp.dot(h.astype(BF16), win_ref[...], preferred_element_type=F32)

    row = lax.broadcasted_iota(jnp.int32, (LANES, LANES), 0)
    col = lax.broadcasted_iota(jnp.int32, (LANES, LANES), 1)
    causal = row >= col
    for g in range(dq // LANES):
        sl = slice(g * LANES, (g + 1) * LANES)
        u = _gelu(proj[:, g * LANES:(g + 1) * LANES])
        vg = _gelu(proj[:, dq + g * LANES:dq + (g + 1) * LANES])
        vg = (_rms(vg) * gmg_ref[:, sl]).astype(BF16)
        wg = jnp.where(causal, ws_ref[g], 0.0).astype(BF16)
        for c in range(tm // LANES):
            rs = slice(c * LANES, (c + 1) * LANES)
            s = jnp.dot(wg, vg[rs], preferred_element_type=F32) + bs_ref[g]
            ya = u[rs] * s
            ya_ref[rs, sl] = (_rms(ya) * og_ref[:, sl]).astype(BF16)
    for hd in range(dq // LANES):
        sl = slice(hd * LANES, (hd + 1) * LANES)
        qh = proj[:, 2 * dq + hd * LANES:2 * dq + (hd + 1) * LANES]
        kh = proj[:, 3 * dq + hd * LANES:3 * dq + (hd + 1) * LANES]
        q_ref[:, sl] = (_rms(qh) * qg_ref[...]).astype(BF16)
        k_ref[:, sl] = (_rms(kh) * kg_ref[...]).astype(BF16)
    v_ref[...] = proj[:, 4 * dq:].astype(BF16)


def _mix_in(x, moe, mod, g1, w_in, gm_g, gm_ws, gm_bs, q_g, k_g, out_g, *, seq, tm):
    n, d = x.shape
    dq = w_in.shape[1] // 5
    ng = dq // LANES
    per_b = seq // tm
    row_spec = pl.BlockSpec((tm, d), lambda i: (i, 0))
    const2 = lambda i: (0, 0)
    in_specs = [row_spec]
    args = [x]
    if moe is not None:
        m, rw, gate2 = moe
        in_specs += [row_spec,
                     pl.BlockSpec((tm, d), lambda i: (n // tm + i, 0)),
                     pl.BlockSpec((tm, rw.shape[1]), lambda i: (i, 0)),
                     pl.BlockSpec((None, 1, d), lambda i: (i // per_b, 0, 0))]
        args += [m, m, rw, gate2]
    in_specs += [pl.BlockSpec((None, N_MOD, 1, d), lambda i: (i // per_b, 0, 0, 0)),
                 pl.BlockSpec((1, d), const2),
                 pl.BlockSpec(w_in.shape, const2),
                 pl.BlockSpec((1, dq), const2),
                 pl.BlockSpec((ng, LANES, LANES), lambda i: (0, 0, 0)),
                 pl.BlockSpec((ng, LANES, 1), lambda i: (0, 0, 0)),
                 pl.BlockSpec((1, LANES), const2),
                 pl.BlockSpec((1, LANES), const2),
                 pl.BlockSpec((1, dq), const2)]
    args += [mod, g1.reshape(1, d), w_in, gm_g.reshape(1, dq), gm_ws, gm_bs.reshape(ng, LANES, 1),
             q_g.reshape(1, LANES), k_g.reshape(1, LANES), out_g.reshape(1, -1)]
    act = jax.ShapeDtypeStruct((n, dq), BF16)
    act_spec = pl.BlockSpec((tm, dq), lambda i: (i, 0))
    out_shape = [act, act, act, act]
    out_specs = [act_spec, act_spec, act_spec, act_spec]
    if moe is not None:
        out_shape = [jax.ShapeDtypeStruct((n, d), F32)] + out_shape
        out_specs = [row_spec] + out_specs
    outs = pl.pallas_call(
        functools.partial(_mix_in_body, moe is not None),
        out_shape=out_shape, grid=(n // tm,), in_specs=in_specs, out_specs=out_specs,
        compiler_params=_params("arbitrary"), name="mix_in",
    )(*args)
    if moe is None:
        return (x,) + tuple(outs)
    return tuple(outs)


def _stick_break_body(q_ref, k_ref, v_ref, og_ref, o_ref, carry_ref, acc_ref):
    tq, dq = q_ref.shape
    nh = dq // LANES
    nsub = tq // LANES
    qi = pl.program_id(1)
    scale = LANES ** -0.5
    r2 = lax.broadcasted_iota(jnp.int32, (LANES, 2 * LANES), 0)
    c2 = lax.broadcasted_iota(jnp.int32, (LANES, 2 * LANES), 1)
    tri_ones = jnp.where((r2 > c2) | (c2 >= LANES), 1.0, 0.0).astype(BF16)

    carry_ref[...] = jnp.zeros_like(carry_ref)
    acc_ref[...] = jnp.zeros_like(acc_ref)

    def key_block(r0, kb, masked):
        rows = tq - r0
        heads = [slice(hd * LANES, (hd + 1) * LANES) for hd in range(nh)]
        start = pl.multiple_of(kb * LANES, LANES)
        if masked:
            rr = lax.broadcasted_iota(jnp.int32, (rows, LANES), 0)
            cc = lax.broadcasted_iota(jnp.int32, (rows, LANES), 1)
            keep = cc < rr
        zs = [lax.dot_general(q_ref[r0:, hs], k_ref[pl.ds(start, LANES), hs],
                              (((1,), (1,)), ((), ())), preferred_element_type=F32) * scale
              for hs in heads]
        lfs, wbs = [], []
        for z in zs:
            lf = -(jnp.maximum(z, 0.0) + jnp.log(1.0 + jnp.exp(-jnp.abs(z))))
            if masked:
                lf = jnp.where(keep, lf, 0.0)
            lfs.append(lf)
            wbs.append(jnp.dot(lf.astype(BF16), tri_ones, preferred_element_type=F32))
        for hd, hs in enumerate(heads):
            carry = carry_ref[hd, r0:, :]
            a = jnp.exp(lfs[hd] + zs[hd] + wbs[hd][:, :LANES] + carry)
            if masked:
                a = jnp.where(keep, a, 0.0)
            acc_ref[hd, r0:, :] += jnp.dot(a.astype(BF16), v_ref[pl.ds(start, LANES), hs],
                                           preferred_element_type=F32)
            carry_ref[hd, r0:, :] = carry + wbs[hd][:, LANES:]

    for j in reversed(range(nsub)):
        key_block(j * LANES, qi * nsub + j, True)

    def more(state):
        kb, live = state
        return jnp.logical_and(kb >= 0, live)

    def step(state):
        kb, _ = state
        key_block(0, kb, False)
        return kb - 1, jnp.max(carry_ref[...]) > LOG_WEIGHT_FLOOR

    lax.while_loop(more, step, (qi * nsub - 1, jnp.max(carry_ref[...]) > LOG_WEIGHT_FLOOR))
    for hd in range(nh):
        hs = slice(hd * LANES, (hd + 1) * LANES)
        o_ref[:, hs] = (_rms(acc_ref[hd]) * og_ref[:, hs]).astype(BF16)


def _stick_break(q, k, v, out_g, *, batch, seq, tq):
    n, dq = q.shape
    nh = dq // LANES
    per_b = seq // tq
    q_spec = pl.BlockSpec((tq, dq), lambda b, i: (b * per_b + i, 0))
    kv_spec = pl.BlockSpec((seq, dq), lambda b, i: (b, 0), pipeline_mode=pl.Buffered(1))
    return pl.pallas_call(
        _stick_break_body,
        out_shape=jax.ShapeDtypeStruct((n, dq), BF16),
        grid=(batch, per_b),
        in_specs=[q_spec, kv_spec, kv_spec, pl.BlockSpec((1, dq), lambda b, i: (0, 1))],
        out_specs=q_spec,
        scratch_shapes=[pltpu.VMEM((nh, tq, LANES), F32), pltpu.VMEM((nh, tq, LANES), F32)],
        compiler_params=_params("arbitrary", "arbitrary"),
        name="stick_break",
    )(q, k, v, out_g.reshape(1, -1))


def _mix_out_body(x_ref, ya_ref, yb_ref, mod_ref, woa_ref, wob_ref, g2_ref, wr_ref, br_ref,
                  x1_ref, h2_ref, ri_ref, rw_ref, cnt_ref, count_scr):
    tm = x_ref.shape[0]

    @pl.when(pl.program_id(0) == 0)
    def _():
        count_scr[...] = jnp.zeros_like(count_scr)

    gate1, shift2, scale2 = mod_ref[2], mod_ref[3], mod_ref[4]
    y = (jnp.dot(ya_ref[...], woa_ref[...], preferred_element_type=F32)
         + jnp.dot(yb_ref[...], wob_ref[...], preferred_element_type=F32))
    x1 = x_ref[...] + gate1 * y
    x1_ref[...] = x1
    h2 = _rms(x1) * g2_ref[...] * (1.0 + scale2) + shift2
    h2_ref[...] = h2

    h_hi = h2.astype(BF16)
    h_lo = (h2 - h_hi.astype(F32)).astype(BF16)
    p = jnp.dot(h_hi, wr_ref[...], preferred_element_type=F32)
    lg = (p[:, :LANES] + p[:, LANES:]
          + jnp.dot(h_lo, wr_ref[:, :LANES], preferred_element_type=F32) + br_ref[...])
    lane = lax.broadcasted_iota(jnp.int32, (tm, LANES), 1)
    lane_f = lane.astype(F32)
    neg = -jnp.inf

    def first_max(vals):
        m = jnp.max(vals, axis=-1, keepdims=True)
        idx = jnp.min(jnp.where(vals == m, lane_f, float(LANES)), axis=-1, keepdims=True)
        return m, idx.astype(jnp.int32)

    is_group = lane < N_GROUPS
    gmax, gidx = first_max(jnp.where(is_group, lg, neg))
    p_group = 1.0 / jnp.sum(jnp.where(is_group, jnp.exp(lg - gmax), 0.0), axis=-1, keepdims=True)
    lo = N_GROUPS + gidx * EXPERTS_PER_GROUP
    el = jnp.where((lane >= lo) & (lane < lo + EXPERTS_PER_GROUP), lg, neg)
    m1, i1 = first_max(el)
    m2, i2 = first_max(jnp.where(lane == i1, neg, el))
    t = jnp.exp(m2 - m1)
    w1 = p_group / (1.0 + t)
    w2 = w1 * t
    e1 = i1 - N_GROUPS
    e2 = i2 - N_GROUPS

    hit1 = lane == e1
    hit2 = lane == e2
    onehot = jnp.where(hit1 | hit2, 1.0, 0.0)
    rr = lax.broadcasted_iota(jnp.int32, (tm, tm), 0)
    cc = lax.broadcasted_iota(jnp.int32, (tm, tm), 1)
    earlier = jnp.where(rr > cc, 1.0, 0.0).astype(BF16)
    base = jnp.dot(earlier, onehot.astype(BF16), preferred_element_type=F32) + count_scr[...]
    r1 = jnp.sum(jnp.where(hit1, base, 0.0), axis=-1, keepdims=True).astype(jnp.int32)
    r2 = jnp.sum(jnp.where(hit2, base, 0.0), axis=-1, keepdims=True).astype(jnp.int32)
    count_scr[...] += jnp.sum(onehot, axis=0, keepdims=True)
    cnt_ref[...] = count_scr[...].astype(jnp.int32)

    l8 = lax.broadcasted_iota(jnp.int32, ri_ref.shape, 1)
    ri_ref[...] = jnp.where(l8 == 0, e1, jnp.where(l8 == 1, e2, jnp.where(l8 == 2, r1, r2)))
    rw_ref[...] = jnp.where(l8 == 0, w1, w2)


def _mix_out(x, ya, yb, mod, w_out, g2, w_router, b_router, *, seq, tm):
    n, d = x.shape
    dq = ya.shape[1]
    per_b = seq // tm
    row_spec = pl.BlockSpec((tm, d), lambda i: (i, 0))
    act_spec = pl.BlockSpec((tm, dq), lambda i: (i, 0))
    small_spec = pl.BlockSpec((tm, 8), lambda i: (i, 0))
    const2 = lambda i: (0, 0)
    return pl.pallas_call(
        _mix_out_body,
        out_shape=[jax.ShapeDtypeStruct((n, d), F32), jax.ShapeDtypeStruct((n, d), F32),
                   jax.ShapeDtypeStruct((n, 8), jnp.int32), jax.ShapeDtypeStruct((n, 8), F32),
                   jax.ShapeDtypeStruct((1, LANES), jnp.int32)],
        grid=(n // tm,),
        in_specs=[row_spec, act_spec, act_spec,
                  pl.BlockSpec((None, N_MOD, 1, d), lambda i: (i // per_b, 0, 0, 0)),
                  pl.BlockSpec((dq, d), const2),
                  pl.BlockSpec((dq, d), lambda i: (1, 0)),
                  pl.BlockSpec((1, d), const2),
                  pl.BlockSpec((d, 2 * LANES), const2),
                  pl.BlockSpec((1, LANES), const2)],
        out_specs=[row_spec, row_spec, small_spec, small_spec,
                   pl.BlockSpec((1, LANES), const2)],
        scratch_shapes=[pltpu.VMEM((1, LANES), F32)],
        compiler_params=_params("arbitrary"),
        name="mix_out",
    )(x, ya, yb, mod, w_out, w_out, g2.reshape(1, d), w_router, b_router)


def _experts_body(n_tok, slot_ref, be_ref, nv_ref, nu_ref, h_hbm, zeros_hbm, wg_ref, wu_ref, wd_ref,
                  out_hbm, code_ref, xbuf, ybuf, gsem, ssem, zsem):
    te = ybuf.shape[0]
    i = pl.program_id(0)
    n_used = nu_ref[0]
    n_assign = slot_ref.shape[0]

    def token_of(code):
        if n_tok & (n_tok - 1) == 0:
            return code & (n_tok - 1)
        return jnp.where(code >= n_tok, code - n_tok, code)

    def start_gather(blk, buf):
        base = blk * te
        for r in range(te):
            tok = token_of(code_ref[base + r])
            pltpu.make_async_copy(h_hbm.at[pl.ds(tok, 1), :], xbuf.at[buf, pl.ds(r, 1), :],
                                  gsem.at[buf]).start(priority=r % 2)

    def wait_scatter(rows):
        for b in reversed(range(te.bit_length())):
            size = 1 << b

            @pl.when((rows & size) != 0)
            def _():
                pltpu.make_async_copy(ybuf.at[pl.ds(0, size), :], out_hbm.at[pl.ds(0, size), :],
                                      ssem).wait()

    @pl.when(i == 0)
    def _():
        fill = pltpu.make_async_copy(zeros_hbm, code_ref, zsem)
        fill.start()
        fill.wait()

        def invert(j, carry):
            for u in range(ROW_GROUP):
                a = j * ROW_GROUP + u
                code_ref[slot_ref[a]] = a
            return carry

        lax.fori_loop(0, n_assign // ROW_GROUP, invert, 0)
        start_gather(0, 0)

    for buf in range(2):
        @pl.when(jnp.logical_and(i + 1 < n_used, (i + 1) % 2 == buf))
        def _():
            start_gather(i + 1, buf)

    @pl.when(i < n_used)
    def _():
        base = i * te
        buf = i % 2
        n_valid = nv_ref[i]
        pltpu.make_async_copy(h_hbm.at[pl.ds(0, te), :], xbuf.at[buf], gsem.at[buf]).wait()
        xb = xbuf[buf].astype(BF16)
        g = jnp.dot(xb, wg_ref[...], preferred_element_type=F32)
        u = jnp.dot(xb, wu_ref[...], preferred_element_type=F32)
        act = (g * jax.nn.sigmoid(g) * u).astype(BF16)

        @pl.when(i > 0)
        def _():
            wait_scatter(nv_ref[jnp.maximum(i - 1, 0)])

        ybuf[...] = jnp.dot(act, wd_ref[...], preferred_element_type=F32)

        def scatter(r, priority):
            dst = code_ref[base + r]
            pltpu.make_async_copy(ybuf.at[pl.ds(r, 1), :], out_hbm.at[pl.ds(dst, 1), :],
                                  ssem).start(priority=priority)

        n_groups = lax.shift_right_logical(n_valid, ROW_GROUP.bit_length() - 1)
        for j in range(te // ROW_GROUP):
            @pl.when(j < n_groups)
            def _():
                for u in range(ROW_GROUP):
                    scatter(j * ROW_GROUP + u, u % 2)

        def scatter_one(r, carry):
            scatter(r, 0)
            return carry

        lax.fori_loop(n_groups * ROW_GROUP, n_valid, scatter_one, 0)

        @pl.when(i == n_used - 1)
        def _():
            wait_scatter(n_valid)


def _experts(h2, slot, block_e, n_valid, n_used, w_gate, w_up, w_down, *, te):
    n, d = h2.shape
    de = w_gate.shape[2]
    n_blocks = block_e.shape[0]
    cap = n_blocks * te
    assert slot.shape[0] % ROW_GROUP == 0 and te % ROW_GROUP == 0
    by_expert = lambda i, slot, be, nv, nu: (be[i], 0, 0)
    grid_spec = pltpu.PrefetchScalarGridSpec(
        num_scalar_prefetch=4,
        grid=(n_blocks,),
        in_specs=[pl.BlockSpec(memory_space=pl.ANY),
                  pl.BlockSpec(memory_space=pl.ANY),
                  pl.BlockSpec((None, d, de), by_expert),
                  pl.BlockSpec((None, d, de), by_expert),
                  pl.BlockSpec((None, de, d), by_expert)],
        out_specs=pl.BlockSpec(memory_space=pl.ANY),
        scratch_shapes=[pltpu.SMEM((cap,), jnp.int32),
                        pltpu.VMEM((2, te, d), F32), pltpu.VMEM((te, d), F32),
                        pltpu.SemaphoreType.DMA((2,)), pltpu.SemaphoreType.DMA(()),
                        pltpu.SemaphoreType.DMA(())],
    )
    return pl.pallas_call(
        functools.partial(_experts_body, n),
        out_shape=jax.ShapeDtypeStruct((TOP_K * n, d), F32),
        grid_spec=grid_spec,
        compiler_params=_params("arbitrary"),
        name="moe_experts",
    )(slot, block_e, n_valid, n_used, h2, jnp.zeros((cap,), jnp.int32), w_gate, w_up, w_down)


def _dispatch(ri, counts, *, te):
    n = ri.shape[0]
    counts = counts[0, :N_EXPERTS]
    padded = (counts + te - 1) // te * te
    pend = jnp.cumsum(padded)
    pstart = pend - padded
    n_blocks = TOP_K * n // te + N_EXPERTS
    slot = (pstart[ri[:, :TOP_K]] + ri[:, TOP_K:2 * TOP_K]).T.reshape(-1)
    block_start = jnp.arange(n_blocks, dtype=jnp.int32) * te
    block_e = jnp.minimum(jnp.sum(pend[None, :] <= block_start[:, None], axis=1),
                          N_EXPERTS - 1).astype(jnp.int32)
    n_valid = jnp.clip(pstart[block_e] + counts[block_e] - block_start, 0, te).astype(jnp.int32)
    n_used = (pend[-1:] // te).astype(jnp.int32)
    return slot.astype(jnp.int32), block_e, n_valid, n_used


def _combine_body(x_ref, m0_ref, m1_ref, rw_ref, g2_ref, o_ref):
    o_ref[...] = x_ref[...] + g2_ref[...] * (rw_ref[:, 0:1] * m0_ref[...]
                                             + rw_ref[:, 1:2] * m1_ref[...])


def _combine(x, m, rw, gate2, *, seq, tm):
    n, d = x.shape
    per_b = seq // tm
    row_spec = pl.BlockSpec((tm, d), lambda i: (i, 0))
    return pl.pallas_call(
        _combine_body,
        out_shape=jax.ShapeDtypeStruct((n, d), F32),
        grid=(n // tm,),
        in_specs=[row_spec, row_spec,
                  pl.BlockSpec((tm, d), lambda i: (n // tm + i, 0)),
                  pl.BlockSpec((tm, rw.shape[1]), lambda i: (i, 0)),
                  pl.BlockSpec((None, 1, d), lambda i: (i // per_b, 0, 0))],
        out_specs=row_spec,
        compiler_params=_params("arbitrary"),
        name="moe_combine",
    )(x, m, m, rw, gate2)


def _tiles(seq):
    tm = min(256, seq)
    tq = min(256, seq)
    te = 256
    return tm, tq, te


def kernel(x, c, w_mod, b_mod, mod_layer, norm1_g, w_in, gm_norm_g, gm_ws, gm_bs, q_norm_g, k_norm_g, out_norm_g, w_out, norm2_g, w_group, b_group, w_route, b_route, w_gate, w_up, w_down):
    batch, seq, d = x.shape
    depth = mod_layer.shape[0]
    n = batch * seq
    tm, tq, te = _tiles(seq)
    assert seq % tm == 0 and seq % tq == 0 and tm % LANES == 0 and tq % LANES == 0
    assert (TOP_K * n) % te == 0 and w_group.shape[2] == N_GROUPS and w_route.shape[2] == N_EXPERTS
    assert w_in.shape[2] * 2 == 5 * w_out.shape[1]

    mod_all = _modulation(c, w_mod, b_mod, mod_layer).reshape(depth, batch, N_MOD, 1, d)
    w_in_b, w_out_b = w_in.astype(BF16), w_out.astype(BF16)
    w_gate_b, w_up_b, w_down_b = w_gate.astype(BF16), w_up.astype(BF16), w_down.astype(BF16)
    pad = LANES - N_GROUPS - N_EXPERTS
    w_router = jnp.pad(jnp.concatenate([w_group, w_route], axis=2), ((0, 0), (0, 0), (0, pad)))
    w_router_hi = w_router.astype(BF16)
    w_router_lo = (w_router - w_router_hi.astype(F32)).astype(BF16)
    w_router = jnp.concatenate([w_router_hi, w_router_lo], axis=2)
    b_router = jnp.pad(jnp.concatenate([b_group, b_route], axis=1), ((0, 0), (0, pad)))

    xf = x.reshape(n, d)
    moe = None
    for l in range(depth):
        xf, ya, q, k, v = _mix_in(xf, moe, mod_all[l], norm1_g[l], w_in_b[l], gm_norm_g[l], gm_ws[l],
                                  gm_bs[l], q_norm_g[l], k_norm_g[l], out_norm_g[l], seq=seq, tm=tm)
        yb = _stick_break(q, k, v, out_norm_g[l], batch=batch, seq=seq, tq=tq)
        xf, h2, ri, rw, counts = _mix_out(xf, ya, yb, mod_all[l], w_out_b[l], norm2_g[l],
                                          w_router[l], b_router[l:l + 1], seq=seq, tm=tm)
        slot, block_e, n_valid, n_used = _dispatch(ri, counts, te=te)
        m = _experts(h2, slot, block_e, n_valid, n_used, w_gate_b[l], w_up_b[l], w_down_b[l], te=te)
        moe = (m, rw, mod_all[l][:, 5])
    out = _combine(xf, *moe, seq=seq, tm=tm)
    return out.reshape(batch, seq, d)
```

```python
import functools

import jax
import jax.numpy as jnp
from jax import lax
from jax.experimental import pallas as pl
from jax.experimental.pallas import tpu as pltpu

F32 = jnp.float32
BF16 = jnp.bfloat16

EPS = 1e-6
LANES = 128
N_MOD = 6
N_GROUPS = 4
EXPERTS_PER_GROUP = 8
N_EXPERTS = N_GROUPS * EXPERTS_PER_GROUP
TOP_K = 2
LOG_WEIGHT_FLOOR = -104.0
ROW_GROUP = 8
VMEM_LIMIT = 56 * 1024 * 1024
HIGH_HALF = 0xFFFF0000


def _rms(x):
    return x * lax.rsqrt(jnp.mean(x * x, axis=-1, keepdims=True) + EPS)


def _gelu(x):
    return 0.5 * x * (1.0 + lax.erf(x * (2.0 ** -0.5)))


def _pack_halves(x):
    half = x.shape[1] // 2
    bits = lax.bitcast_convert_type(x.astype(BF16).astype(F32), jnp.uint32)
    return (lax.shift_right_logical(bits[:, :half], jnp.uint32(16))
            | (bits[:, half:] & jnp.uint32(HIGH_HALF)))


def _unpack_halves(w):
    lo = lax.bitcast_convert_type(lax.shift_left(w, jnp.uint32(16)), F32)
    hi = lax.bitcast_convert_type(w & jnp.uint32(HIGH_HALF), F32)
    return lo, hi


def _moe_residual(x, m0_ref, m1_ref, rw_ref, g2_ref):
    lo0, hi0 = _unpack_halves(m0_ref[...])
    lo1, hi1 = _unpack_halves(m1_ref[...])
    w1, w2 = rw_ref[:, 0:1], rw_ref[:, 1:2]
    m = jnp.concatenate([w1 * lo0 + w2 * lo1, w1 * hi0 + w2 * hi1], axis=-1)
    return x + g2_ref[...] * m


def _params(*sem):
    return pltpu.CompilerParams(dimension_semantics=sem, vmem_limit_bytes=VMEM_LIMIT)


def _mod_body(c_ref, w_ref, b_ref, ml_ref, o_ref):
    c = c_ref[...]
    sc = c * jax.nn.sigmoid(c)
    r = jnp.dot(sc, w_ref[...], preferred_element_type=F32,
                precision=lax.Precision.HIGHEST) + b_ref[...]
    for l in range(o_ref.shape[0]):
        o_ref[l] = r + ml_ref[l:l + 1, :]


def _modulation(c, w_mod, b_mod, mod_layer):
    b, d = c.shape
    depth, w = mod_layer.shape
    tn = 1024
    return pl.pallas_call(
        _mod_body,
        out_shape=jax.ShapeDtypeStruct((depth, b, w), F32),
        grid=(w // tn,),
        in_specs=[pl.BlockSpec((b, d), lambda j: (0, 0)),
                  pl.BlockSpec((d, tn), lambda j: (0, j)),
                  pl.BlockSpec((1, tn), lambda j: (0, j)),
                  pl.BlockSpec((depth, tn), lambda j: (0, j))],
        out_specs=pl.BlockSpec((depth, b, tn), lambda j: (0, 0, j)),
        compiler_params=_params("arbitrary"),
        name="modulation",
    )(c, w_mod, b_mod.reshape(1, w), mod_layer)


def _mix_in_body(has_moe, *refs):
    refs = list(refs)
    x_ref = refs.pop(0)
    if has_moe:
        m0_ref, m1_ref, rw_ref, g2_ref = refs.pop(0), refs.pop(0), refs.pop(0), refs.pop(0)
    (mod_ref, g1_ref, win_ref, gmg_ref, ws_ref, bs_ref, qg_ref, kg_ref, og_ref) = refs[:9]
    refs = refs[9:]
    if has_moe:
        xo_ref = refs.pop(0)
    ya_ref, q_ref, k_ref, v_ref = refs

    x = x_ref[...]
    if has_moe:
        x = _moe_residual(x, m0_ref, m1_ref, rw_ref, g2_ref)
        xo_ref[...] = x
    tm = x.shape[0]
    dq = ya_ref.shape[1]
    shift1, scale1 = mod_ref[0], mod_ref[1]
    h = _rms(x) * g1_ref[...] * (1.0 + scale1) + shift1
    proj = jnp.dot(h.astype(BF16), win_ref[...], preferred_element_type=F32)

    row = lax.broadcasted_iota(jnp.int32, (LANES, LANES), 0)
    col = lax.broadcasted_iota(jnp.int32, (LANES, LANES), 1)
    causal = row >= col
    for g in range(dq // LANES):
        sl = slice(g * LANES, (g + 1) * LANES)
        u = _gelu(proj[:, g * LANES:(g + 1) * LANES])
        vg = _gelu(proj[:, dq + g * LANES:dq + (g + 1) * LANES])
        vg = (_rms(vg) * gmg_ref[:, sl]).astype(BF16)
        wg = jnp.where(causal, ws_ref[g], 0.0).astype(BF16)
        for c in range(tm // LANES):
            rs = slice(c * LANES, (c + 1) * LANES)
            s = jnp.dot(wg, vg[rs], preferred_element_type=F32) + bs_ref[g]
            ya = u[rs] * s
            ya_ref[rs, sl] = (_rms(ya) * og_ref[:, sl]).astype(BF16)
    for hd in range(dq // LANES):
        sl = slice(hd * LANES, (hd + 1) * LANES)
        qh = proj[:, 2 * dq + hd * LANES:2 * dq + (hd + 1) * LANES]
        kh = proj[:, 3 * dq + hd * LANES:3 * dq + (hd + 1) * LANES]
        q_ref[:, sl] = (_rms(qh) * qg_ref[...]).astype(BF16)
        k_ref[:, sl] = (_rms(kh) * kg_ref[...]).astype(BF16)
    v_ref[...] = proj[:, 4 * dq:].astype(BF16)


def _mix_in(x, moe, mod, g1, w_in, gm_g, gm_ws, gm_bs, q_g, k_g, out_g, *, seq, tm):
    n, d = x.shape
    dq = w_in.shape[1] // 5
    ng = dq // LANES
    per_b = seq // tm
    row_spec = pl.BlockSpec((tm, d), lambda i: (i, 0))
    const2 = lambda i: (0, 0)
    in_specs = [row_spec]
    args = [x]
    if moe is not None:
        m, rw, gate2 = moe
        in_specs += [pl.BlockSpec((tm, d // 2), lambda i: (i, 0)),
                     pl.BlockSpec((tm, d // 2), lambda i: (n // tm + i, 0)),
                     pl.BlockSpec((tm, rw.shape[1]), lambda i: (i, 0)),
                     pl.BlockSpec((None, 1, d), lambda i: (i // per_b, 0, 0))]
        args += [m, m, rw, gate2]
    in_specs += [pl.BlockSpec((None, N_MOD, 1, d), lambda i: (i // per_b, 0, 0, 0)),
                 pl.BlockSpec((1, d), const2),
                 pl.BlockSpec(w_in.shape, const2),
                 pl.BlockSpec((1, dq), const2),
                 pl.BlockSpec((ng, LANES, LANES), lambda i: (0, 0, 0)),
                 pl.BlockSpec((ng, LANES, 1), lambda i: (0, 0, 0)),
                 pl.BlockSpec((1, LANES), const2),
                 pl.BlockSpec((1, LANES), const2),
                 pl.BlockSpec((1, dq), const2)]
    args += [mod, g1.reshape(1, d), w_in, gm_g.reshape(1, dq), gm_ws, gm_bs.reshape(ng, LANES, 1),
             q_g.reshape(1, LANES), k_g.reshape(1, LANES), out_g.reshape(1, -1)]
    act = jax.ShapeDtypeStruct((n, dq), BF16)
    act_spec = pl.BlockSpec((tm, dq), lambda i: (i, 0))
    out_shape = [act, act, act, act]
    out_specs = [act_spec, act_spec, act_spec, act_spec]
    if moe is not None:
        out_shape = [jax.ShapeDtypeStruct((n, d), F32)] + out_shape
        out_specs = [row_spec] + out_specs
    outs = pl.pallas_call(
        functools.partial(_mix_in_body, moe is not None),
        out_shape=out_shape, grid=(n // tm,), in_specs=in_specs, out_specs=out_specs,
        compiler_params=_params("arbitrary"), name="mix_in",
    )(*args)
    if moe is None:
        return (x,) + tuple(outs)
    return tuple(outs)


def _stick_break_body(q_ref, k_ref, v_ref, og_ref, o_ref, carry_ref, acc_ref):
    tq, dq = q_ref.shape
    nh = dq // LANES
    nsub = tq // LANES
    qi = pl.program_id(1)
    scale = LANES ** -0.5
    r2 = lax.broadcasted_iota(jnp.int32, (LANES, 2 * LANES), 0)
    c2 = lax.broadcasted_iota(jnp.int32, (LANES, 2 * LANES), 1)
    tri_ones = jnp.where((r2 > c2) | (c2 >= LANES), 1.0, 0.0).astype(BF16)

    carry_ref[...] = jnp.zeros_like(carry_ref)
    acc_ref[...] = jnp.zeros_like(acc_ref)

    def key_block(r0, kb, masked):
        rows = tq - r0
        heads = [slice(hd * LANES, (hd + 1) * LANES) for hd in range(nh)]
        start = pl.multiple_of(kb * LANES, LANES)
        if masked:
            rr = lax.broadcasted_iota(jnp.int32, (rows, LANES), 0)
            cc = lax.broadcasted_iota(jnp.int32, (rows, LANES), 1)
            keep = cc < rr
        zs = [lax.dot_general(q_ref[r0:, hs], k_ref[pl.ds(start, LANES), hs],
                              (((1,), (1,)), ((), ())), preferred_element_type=F32) * scale
              for hs in heads]
        lfs, wbs = [], []
        for z in zs:
            lf = -(jnp.maximum(z, 0.0) + jnp.log(1.0 + jnp.exp(-jnp.abs(z))))
            if masked:
                lf = jnp.where(keep, lf, 0.0)
            lfs.append(lf)
            wbs.append(jnp.dot(lf.astype(BF16), tri_ones, preferred_element_type=F32))
        for hd, hs in enumerate(heads):
            carry = carry_ref[hd, r0:, :]
            a = jnp.exp(lfs[hd] + zs[hd] + wbs[hd][:, :LANES] + carry)
            if masked:
                a = jnp.where(keep, a, 0.0)
            acc_ref[hd, r0:, :] += jnp.dot(a.astype(BF16), v_ref[pl.ds(start, LANES), hs],
                                           preferred_element_type=F32)
            carry_ref[hd, r0:, :] = carry + wbs[hd][:, LANES:]

    for j in reversed(range(nsub)):
        key_block(j * LANES, qi * nsub + j, True)

    def more(state):
        kb, live = state
        return jnp.logical_and(kb >= 0, live)

    def step(state):
        kb, _ = state
        key_block(0, kb, False)
        return kb - 1, jnp.max(carry_ref[...]) > LOG_WEIGHT_FLOOR

    lax.while_loop(more, step, (qi * nsub - 1, jnp.max(carry_ref[...]) > LOG_WEIGHT_FLOOR))
    for hd in range(nh):
        hs = slice(hd * LANES, (hd + 1) * LANES)
        o_ref[:, hs] = (_rms(acc_ref[hd]) * og_ref[:, hs]).astype(BF16)


def _stick_break(q, k, v, out_g, *, batch, seq, tq):
    n, dq = q.shape
    nh = dq // LANES
    per_b = seq // tq
    q_spec = pl.BlockSpec((tq, dq), lambda b, i: (b * per_b + i, 0))
    kv_spec = pl.BlockSpec((seq, dq), lambda b, i: (b, 0), pipeline_mode=pl.Buffered(1))
    return pl.pallas_call(
        _stick_break_body,
        out_shape=jax.ShapeDtypeStruct((n, dq), BF16),
        grid=(batch, per_b),
        in_specs=[q_spec, kv_spec, kv_spec, pl.BlockSpec((1, dq), lambda b, i: (0, 1))],
        out_specs=q_spec,
        scratch_shapes=[pltpu.VMEM((nh, tq, LANES), F32), pltpu.VMEM((nh, tq, LANES), F32)],
        compiler_params=_params("arbitrary", "arbitrary"),
        name="stick_break",
    )(q, k, v, out_g.reshape(1, -1))


def _mix_out_body(x_ref, ya_ref, yb_ref, mod_ref, woa_ref, wob_ref, g2_ref, wr_ref, br_ref,
                  x1_ref, h2_ref, ri_ref, rw_ref, cnt_ref, count_scr):
    tm = x_ref.shape[0]

    @pl.when(pl.program_id(0) == 0)
    def _():
        count_scr[...] = jnp.zeros_like(count_scr)

    gate1, shift2, scale2 = mod_ref[2], mod_ref[3], mod_ref[4]
    y = (jnp.dot(ya_ref[...], woa_ref[...], preferred_element_type=F32)
         + jnp.dot(yb_ref[...], wob_ref[...], preferred_element_type=F32))
    x1 = x_ref[...] + gate1 * y
    x1_ref[...] = x1
    h2 = _rms(x1) * g2_ref[...] * (1.0 + scale2) + shift2
    h2_ref[...] = _pack_halves(h2)

    h_hi = h2.astype(BF16)
    h_lo = (h2 - h_hi.astype(F32)).astype(BF16)
    p = jnp.dot(h_hi, wr_ref[...], preferred_element_type=F32)
    lg = (p[:, :LANES] + p[:, LANES:]
          + jnp.dot(h_lo, wr_ref[:, :LANES], preferred_element_type=F32) + br_ref[...])
    lane = lax.broadcasted_iota(jnp.int32, (tm, LANES), 1)
    lane_f = lane.astype(F32)
    neg = -jnp.inf

    def first_max(vals):
        m = jnp.max(vals, axis=-1, keepdims=True)
        idx = jnp.min(jnp.where(vals == m, lane_f, float(LANES)), axis=-1, keepdims=True)
        return m, idx.astype(jnp.int32)

    is_group = lane < N_GROUPS
    gmax, gidx = first_max(jnp.where(is_group, lg, neg))
    p_group = 1.0 / jnp.sum(jnp.where(is_group, jnp.exp(lg - gmax), 0.0), axis=-1, keepdims=True)
    lo = N_GROUPS + gidx * EXPERTS_PER_GROUP
    el = jnp.where((lane >= lo) & (lane < lo + EXPERTS_PER_GROUP), lg, neg)
    m1, i1 = first_max(el)
    m2, i2 = first_max(jnp.where(lane == i1, neg, el))
    t = jnp.exp(m2 - m1)
    w1 = p_group / (1.0 + t)
    w2 = w1 * t
    e1 = i1 - N_GROUPS
    e2 = i2 - N_GROUPS

    hit1 = lane == e1
    hit2 = lane == e2
    onehot = jnp.where(hit1 | hit2, 1.0, 0.0)
    rr = lax.broadcasted_iota(jnp.int32, (tm, tm), 0)
    cc = lax.broadcasted_iota(jnp.int32, (tm, tm), 1)
    earlier = jnp.where(rr > cc, 1.0, 0.0).astype(BF16)
    base = jnp.dot(earlier, onehot.astype(BF16), preferred_element_type=F32) + count_scr[...]
    r1 = jnp.sum(jnp.where(hit1, base, 0.0), axis=-1, keepdims=True).astype(jnp.int32)
    r2 = jnp.sum(jnp.where(hit2, base, 0.0), axis=-1, keepdims=True).astype(jnp.int32)
    count_scr[...] += jnp.sum(onehot, axis=0, keepdims=True)
    cnt_ref[...] = count_scr[...].astype(jnp.int32)

    l8 = lax.broadcasted_iota(jnp.int32, ri_ref.shape, 1)
    ri_ref[...] = jnp.where(l8 == 0, e1, jnp.where(l8 == 1, e2, jnp.where(l8 == 2, r1, r2)))
    rw_ref[...] = jnp.where(l8 == 0, w1, w2)


def _mix_out(x, ya, yb, mod, w_out, g2, w_router, b_router, *, seq, tm):
    n, d = x.shape
    dq = ya.shape[1]
    per_b = seq // tm
    row_spec = pl.BlockSpec((tm, d), lambda i: (i, 0))
    act_spec = pl.BlockSpec((tm, dq), lambda i: (i, 0))
    small_spec = pl.BlockSpec((tm, 8), lambda i: (i, 0))
    const2 = lambda i: (0, 0)
    return pl.pallas_call(
        _mix_out_body,
        out_shape=[jax.ShapeDtypeStruct((n, d), F32), jax.ShapeDtypeStruct((n, d // 2), jnp.uint32),
                   jax.ShapeDtypeStruct((n, 8), jnp.int32), jax.ShapeDtypeStruct((n, 8), F32),
                   jax.ShapeDtypeStruct((1, LANES), jnp.int32)],
        grid=(n // tm,),
        in_specs=[row_spec, act_spec, act_spec,
                  pl.BlockSpec((None, N_MOD, 1, d), lambda i: (i // per_b, 0, 0, 0)),
                  pl.BlockSpec((dq, d), const2),
                  pl.BlockSpec((dq, d), lambda i: (1, 0)),
                  pl.BlockSpec((1, d), const2),
                  pl.BlockSpec((d, 2 * LANES), const2),
                  pl.BlockSpec((1, LANES), const2)],
        out_specs=[row_spec, pl.BlockSpec((tm, d // 2), lambda i: (i, 0)), small_spec, small_spec,
                   pl.BlockSpec((1, LANES), const2)],
        scratch_shapes=[pltpu.VMEM((1, LANES), F32)],
        compiler_params=_params("arbitrary"),
        name="mix_out",
    )(x, ya, yb, mod, w_out, w_out, g2.reshape(1, d), w_router, b_router)


def _experts_body(n_tok, slot_ref, be_ref, nv_ref, nu_ref, h_hbm, zeros_hbm, wg_ref, wu_ref, wd_ref,
                  out_hbm, code_ref, xbuf, ybuf, wg_b, wu_b, wd_b, gsem, ssem, zsem):
    te = ybuf.shape[0]
    i = pl.program_id(0)
    n_used = nu_ref[0]
    n_assign = slot_ref.shape[0]

    def token_of(code):
        if n_tok & (n_tok - 1) == 0:
            return code & (n_tok - 1)
        return jnp.where(code >= n_tok, code - n_tok, code)

    def start_gather(blk, buf):
        base = blk * te
        for r in range(te):
            tok = token_of(code_ref[base + r])
            pltpu.make_async_copy(h_hbm.at[pl.ds(tok, 1), :], xbuf.at[buf, pl.ds(r, 1), :],
                                  gsem.at[buf]).start(priority=r % 2)

    def wait_scatter(rows):
        for b in reversed(range(te.bit_length())):
            size = 1 << b

            @pl.when((rows & size) != 0)
            def _():
                pltpu.make_async_copy(ybuf.at[pl.ds(0, size), :], out_hbm.at[pl.ds(0, size), :],
                                      ssem).wait()

    @pl.when(i == 0)
    def _():
        fill = pltpu.make_async_copy(zeros_hbm, code_ref, zsem)
        fill.start()
        fill.wait()

        def invert(j, carry):
            for u in range(ROW_GROUP):
                a = j * ROW_GROUP + u
                code_ref[slot_ref[a]] = a
            return carry

        lax.fori_loop(0, n_assign // ROW_GROUP, invert, 0)
        start_gather(0, 0)

    for buf in range(2):
        @pl.when(jnp.logical_and(i + 1 < n_used, (i + 1) % 2 == buf))
        def _():
            start_gather(i + 1, buf)

    @pl.when(i < n_used)
    def _():
        base = i * te
        buf = i % 2
        n_valid = nv_ref[i]
        @pl.when(jnp.logical_or(i == 0, be_ref[i] != be_ref[jnp.maximum(i - 1, 0)]))
        def _():
            wg_b[...] = wg_ref[...].astype(BF16)
            wu_b[...] = wu_ref[...].astype(BF16)
            wd_b[...] = wd_ref[...].astype(BF16)

        pltpu.make_async_copy(h_hbm.at[pl.ds(0, te), :], xbuf.at[buf], gsem.at[buf]).wait()
        half = xbuf.shape[2]
        x_lo, x_hi = _unpack_halves(xbuf[buf])
        x_lo, x_hi = x_lo.astype(BF16), x_hi.astype(BF16)
        g = (jnp.dot(x_lo, wg_b[:half], preferred_element_type=F32)
             + jnp.dot(x_hi, wg_b[half:], preferred_element_type=F32))
        u = (jnp.dot(x_lo, wu_b[:half], preferred_element_type=F32)
             + jnp.dot(x_hi, wu_b[half:], preferred_element_type=F32))
        act = (g * jax.nn.sigmoid(g) * u).astype(BF16)

        @pl.when(i > 0)
        def _():
            wait_scatter(nv_ref[jnp.maximum(i - 1, 0)])

        ybuf[...] = _pack_halves(jnp.dot(act, wd_b[...], preferred_element_type=F32))

        def scatter(r, priority):
            dst = code_ref[base + r]
            pltpu.make_async_copy(ybuf.at[pl.ds(r, 1), :], out_hbm.at[pl.ds(dst, 1), :],
                                  ssem).start(priority=priority)

        n_groups = lax.shift_right_logical(n_valid, ROW_GROUP.bit_length() - 1)
        for j in range(te // ROW_GROUP):
            @pl.when(j < n_groups)
            def _():
                for u in range(ROW_GROUP):
                    scatter(j * ROW_GROUP + u, u % 2)

        def scatter_one(r, carry):
            scatter(r, 0)
            return carry

        lax.fori_loop(n_groups * ROW_GROUP, n_valid, scatter_one, 0)

        @pl.when(i == n_used - 1)
        def _():
            wait_scatter(n_valid)


def _experts(h2, slot, block_e, n_valid, n_used, w_gate, w_up, w_down, *, layer, te):
    n, half = h2.shape
    d = 2 * half
    de = w_gate.shape[3]
    n_blocks = block_e.shape[0]
    cap = n_blocks * te
    assert slot.shape[0] % ROW_GROUP == 0 and te % ROW_GROUP == 0
    by_expert = lambda i, slot, be, nv, nu: (layer, be[i], 0, 0)
    grid_spec = pltpu.PrefetchScalarGridSpec(
        num_scalar_prefetch=4,
        grid=(n_blocks,),
        in_specs=[pl.BlockSpec(memory_space=pl.ANY),
                  pl.BlockSpec(memory_space=pl.ANY),
                  pl.BlockSpec((None, None, d, de), by_expert),
                  pl.BlockSpec((None, None, d, de), by_expert),
                  pl.BlockSpec((None, None, de, d), by_expert)],
        out_specs=pl.BlockSpec(memory_space=pl.ANY),
        scratch_shapes=[pltpu.SMEM((cap,), jnp.int32),
                        pltpu.VMEM((2, te, half), jnp.uint32), pltpu.VMEM((te, half), jnp.uint32),
                        pltpu.VMEM((d, de), BF16), pltpu.VMEM((d, de), BF16),
                        pltpu.VMEM((de, d), BF16),
                        pltpu.SemaphoreType.DMA((2,)), pltpu.SemaphoreType.DMA(()),
                        pltpu.SemaphoreType.DMA(())],
    )
    return pl.pallas_call(
        functools.partial(_experts_body, n),
        out_shape=jax.ShapeDtypeStruct((TOP_K * n, half), jnp.uint32),
        grid_spec=grid_spec,
        compiler_params=_params("arbitrary"),
        name="moe_experts",
    )(slot, block_e, n_valid, n_used, h2, jnp.zeros((cap,), jnp.int32), w_gate, w_up, w_down)


def _dispatch(ri, counts, *, te):
    n = ri.shape[0]
    counts = counts[0, :N_EXPERTS]
    padded = (counts + te - 1) // te * te
    pend = jnp.cumsum(padded)
    pstart = pend - padded
    n_blocks = TOP_K * n // te + N_EXPERTS
    slot = (pstart[ri[:, :TOP_K]] + ri[:, TOP_K:2 * TOP_K]).T.reshape(-1)
    block_start = jnp.arange(n_blocks, dtype=jnp.int32) * te
    block_e = jnp.minimum(jnp.sum(pend[None, :] <= block_start[:, None], axis=1),
                          N_EXPERTS - 1).astype(jnp.int32)
    n_valid = jnp.clip(pstart[block_e] + counts[block_e] - block_start, 0, te).astype(jnp.int32)
    n_used = (pend[-1:] // te).astype(jnp.int32)
    return slot.astype(jnp.int32), block_e, n_valid, n_used


def _combine_body(x_ref, m0_ref, m1_ref, rw_ref, g2_ref, o_ref):
    o_ref[...] = _moe_residual(x_ref[...], m0_ref, m1_ref, rw_ref, g2_ref)


def _combine(x, m, rw, gate2, *, seq, tm):
    n, d = x.shape
    per_b = seq // tm
    row_spec = pl.BlockSpec((tm, d), lambda i: (i, 0))
    return pl.pallas_call(
        _combine_body,
        out_shape=jax.ShapeDtypeStruct((n, d), F32),
        grid=(n // tm,),
        in_specs=[row_spec,
                  pl.BlockSpec((tm, d // 2), lambda i: (i, 0)),
                  pl.BlockSpec((tm, d // 2), lambda i: (n // tm + i, 0)),
                  pl.BlockSpec((tm, rw.shape[1]), lambda i: (i, 0)),
                  pl.BlockSpec((None, 1, d), lambda i: (i // per_b, 0, 0))],
        out_specs=row_spec,
        compiler_params=_params("arbitrary"),
        name="moe_combine",
    )(x, m, m, rw, gate2)


def _tiles(seq):
    tm = min(256, seq)
    tq = min(256, seq)
    te = 256
    return tm, tq, te


def kernel(x, c, w_mod, b_mod, mod_layer, norm1_g, w_in, gm_norm_g, gm_ws, gm_bs, q_norm_g, k_norm_g, out_norm_g, w_out, norm2_g, w_group, b_group, w_route, b_route, w_gate, w_up, w_down):
    batch, seq, d = x.shape
    depth = mod_layer.shape[0]
    n = batch * seq
    tm, tq, te = _tiles(seq)
    assert seq % tm == 0 and seq % tq == 0 and tm % LANES == 0 and tq % LANES == 0
    assert (TOP_K * n) % te == 0 and w_group.shape[2] == N_GROUPS and w_route.shape[2] == N_EXPERTS
    assert w_in.shape[2] * 2 == 5 * w_out.shape[1]

    mod_all = _modulation(c, w_mod, b_mod, mod_layer).reshape(depth, batch, N_MOD, 1, d)
    w_in_b, w_out_b = w_in.astype(BF16), w_out.astype(BF16)
    pad = LANES - N_GROUPS - N_EXPERTS
    w_router = jnp.pad(jnp.concatenate([w_group, w_route], axis=2), ((0, 0), (0, 0), (0, pad)))
    w_router_hi = w_router.astype(BF16)
    w_router_lo = (w_router - w_router_hi.astype(F32)).astype(BF16)
    w_router = jnp.concatenate([w_router_hi, w_router_lo], axis=2)
    b_router = jnp.pad(jnp.concatenate([b_group, b_route], axis=1), ((0, 0), (0, pad)))

    xf = x.reshape(n, d)
    moe = None
    for l in range(depth):
        xf, ya, q, k, v = _mix_in(xf, moe, mod_all[l], norm1_g[l], w_in_b[l], gm_norm_g[l], gm_ws[l],
                                  gm_bs[l], q_norm_g[l], k_norm_g[l], out_norm_g[l], seq=seq, tm=tm)
        yb = _stick_break(q, k, v, out_norm_g[l], batch=batch, seq=seq, tq=tq)
        xf, h2, ri, rw, counts = _mix_out(xf, ya, yb, mod_all[l], w_out_b[l], norm2_g[l],
                                          w_router[l], b_router[l:l + 1], seq=seq, tm=tm)
        slot, block_e, n_valid, n_used = _dispatch(ri, counts, te=te)
        m = _experts(h2, slot, block_e, n_valid, n_used, w_gate, w_up, w_down, layer=l, te=te)
        moe = (m, rw, mod_all[l][:, 5])
    out = _combine(xf, *moe, seq=seq, tm=tm)
    return out.reshape(batch, seq, d)
```

```python
import functools

import jax
import jax.numpy as jnp
from jax import lax
from jax.experimental import pallas as pl
from jax.experimental.pallas import tpu as pltpu

F32 = jnp.float32
BF16 = jnp.bfloat16

EPS = 1e-6
LANES = 128
N_MOD = 6
N_GROUPS = 4
EXPERTS_PER_GROUP = 8
N_EXPERTS = N_GROUPS * EXPERTS_PER_GROUP
TOP_K = 2
LOG_WEIGHT_FLOOR = -104.0
ROW_GROUP = 8
VMEM_LIMIT = 56 * 1024 * 1024
HIGH_HALF = 0xFFFF0000


def _rms(x):
    return x * lax.rsqrt(jnp.mean(x * x, axis=-1, keepdims=True) + EPS)


def _gelu(x):
    return 0.5 * x * (1.0 + lax.erf(x * (2.0 ** -0.5)))


def _pack_halves(x):
    half = x.shape[1] // 2
    bits = lax.bitcast_convert_type(x.astype(BF16).astype(F32), jnp.uint32)
    return (lax.shift_right_logical(bits[:, :half], jnp.uint32(16))
            | (bits[:, half:] & jnp.uint32(HIGH_HALF)))


def _unpack_halves(w):
    lo = lax.bitcast_convert_type(lax.shift_left(w, jnp.uint32(16)), F32)
    hi = lax.bitcast_convert_type(w & jnp.uint32(HIGH_HALF), F32)
    return lo, hi


def _store_token_major(ref, base, packed):
    rows, width = packed.shape
    nch = width // LANES
    for c in range(nch):
        ref[pl.ds(base + c, rows, stride=nch), :] = packed[:, c * LANES:(c + 1) * LANES]


def _load_token_major(ref, base, rows, nch):
    return jnp.concatenate([ref[pl.ds(base + c, rows, stride=nch), :] for c in range(nch)], axis=-1)


def _moe_residual(x, m0_ref, m1_ref, rw_ref, g2_ref):
    tm, d = x.shape
    nch = d // 2 // LANES
    lo0, hi0 = _unpack_halves(_load_token_major(m0_ref, 0, tm, nch))
    lo1, hi1 = _unpack_halves(_load_token_major(m1_ref, 0, tm, nch))
    w1, w2 = rw_ref[:, 0:1], rw_ref[:, 1:2]
    m = jnp.concatenate([w1 * lo0 + w2 * lo1, w1 * hi0 + w2 * hi1], axis=-1)
    return x + g2_ref[...] * m


def _params(*sem):
    return pltpu.CompilerParams(dimension_semantics=sem, vmem_limit_bytes=VMEM_LIMIT)


def _mod_body(c_ref, w_ref, b_ref, ml_ref, o_ref):
    c = c_ref[...]
    sc = c * jax.nn.sigmoid(c)
    r = jnp.dot(sc, w_ref[...], preferred_element_type=F32,
                precision=lax.Precision.HIGHEST) + b_ref[...]
    for l in range(o_ref.shape[0]):
        o_ref[l] = r + ml_ref[l:l + 1, :]


def _modulation(c, w_mod, b_mod, mod_layer):
    b, d = c.shape
    depth, w = mod_layer.shape
    tn = 1024
    return pl.pallas_call(
        _mod_body,
        out_shape=jax.ShapeDtypeStruct((depth, b, w), F32),
        grid=(w // tn,),
        in_specs=[pl.BlockSpec((b, d), lambda j: (0, 0)),
                  pl.BlockSpec((d, tn), lambda j: (0, j)),
                  pl.BlockSpec((1, tn), lambda j: (0, j)),
                  pl.BlockSpec((depth, tn), lambda j: (0, j))],
        out_specs=pl.BlockSpec((depth, b, tn), lambda j: (0, 0, j)),
        compiler_params=_params("arbitrary"),
        name="modulation",
    )(c, w_mod, b_mod.reshape(1, w), mod_layer)


def _mix_in_body(has_moe, *refs):
    refs = list(refs)
    x_ref = refs.pop(0)
    if has_moe:
        m0_ref, m1_ref, rw_ref, g2_ref = refs.pop(0), refs.pop(0), refs.pop(0), refs.pop(0)
    (mod_ref, g1_ref, win_ref, gmg_ref, ws_ref, bs_ref, qg_ref, kg_ref, og_ref) = refs[:9]
    refs = refs[9:]
    if has_moe:
        xo_ref = refs.pop(0)
    ya_ref, q_ref, k_ref, v_ref = refs

    x = x_ref[...]
    if has_moe:
        x = _moe_residual(x, m0_ref, m1_ref, rw_ref, g2_ref)
        xo_ref[...] = x
    tm = x.shape[0]
    dq = ya_ref.shape[1]
    shift1, scale1 = mod_ref[0], mod_ref[1]
    h = _rms(x) * g1_ref[...] * (1.0 + scale1) + shift1
    proj = jnp.dot(h.astype(BF16), win_ref[...], preferred_element_type=F32)

    row = lax.broadcasted_iota(jnp.int32, (LANES, LANES), 0)
    col = lax.broadcasted_iota(jnp.int32, (LANES, LANES), 1)
    causal = row >= col
    for g in range(dq // LANES):
        sl = slice(g * LANES, (g + 1) * LANES)
        u = _gelu(proj[:, g * LANES:(g + 1) * LANES])
        vg = _gelu(proj[:, dq + g * LANES:dq + (g + 1) * LANES])
        vg = (_rms(vg) * gmg_ref[:, sl]).astype(BF16)
        wg = jnp.where(causal, ws_ref[g], 0.0).astype(BF16)
        for c in range(tm // LANES):
            rs = slice(c * LANES, (c + 1) * LANES)
            s = jnp.dot(wg, vg[rs], preferred_element_type=F32) + bs_ref[g]
            ya = u[rs] * s
            ya_ref[rs, sl] = (_rms(ya) * og_ref[:, sl]).astype(BF16)
    for hd in range(dq // LANES):
        sl = slice(hd * LANES, (hd + 1) * LANES)
        qh = proj[:, 2 * dq + hd * LANES:2 * dq + (hd + 1) * LANES]
        kh = proj[:, 3 * dq + hd * LANES:3 * dq + (hd + 1) * LANES]
        q_ref[:, sl] = (_rms(qh) * qg_ref[...]).astype(BF16)
        k_ref[:, sl] = (_rms(kh) * kg_ref[...]).astype(BF16)
    v_ref[...] = proj[:, 4 * dq:].astype(BF16)


def _mix_in(x, moe, mod, g1, w_in, gm_g, gm_ws, gm_bs, q_g, k_g, out_g, *, seq, tm):
    n, d = x.shape
    dq = w_in.shape[1] // 5
    ng = dq // LANES
    per_b = seq // tm
    row_spec = pl.BlockSpec((tm, d), lambda i: (i, 0))
    const2 = lambda i: (0, 0)
    in_specs = [row_spec]
    args = [x]
    if moe is not None:
        m, rw, gate2 = moe
        nch = d // 2 // LANES
        in_specs += [pl.BlockSpec((tm * nch, LANES), lambda i: (i, 0)),
                     pl.BlockSpec((tm * nch, LANES), lambda i: (n // tm + i, 0)),
                     pl.BlockSpec((tm, rw.shape[1]), lambda i: (i, 0)),
                     pl.BlockSpec((None, 1, d), lambda i: (i // per_b, 0, 0))]
        args += [m, m, rw, gate2]
    in_specs += [pl.BlockSpec((None, N_MOD, 1, d), lambda i: (i // per_b, 0, 0, 0)),
                 pl.BlockSpec((1, d), const2),
                 pl.BlockSpec(w_in.shape, const2),
                 pl.BlockSpec((1, dq), const2),
                 pl.BlockSpec((ng, LANES, LANES), lambda i: (0, 0, 0)),
                 pl.BlockSpec((ng, LANES, 1), lambda i: (0, 0, 0)),
                 pl.BlockSpec((1, LANES), const2),
                 pl.BlockSpec((1, LANES), const2),
                 pl.BlockSpec((1, dq), const2)]
    args += [mod, g1.reshape(1, d), w_in, gm_g.reshape(1, dq), gm_ws, gm_bs.reshape(ng, LANES, 1),
             q_g.reshape(1, LANES), k_g.reshape(1, LANES), out_g.reshape(1, -1)]
    act = jax.ShapeDtypeStruct((n, dq), BF16)
    act_spec = pl.BlockSpec((tm, dq), lambda i: (i, 0))
    out_shape = [act, act, act, act]
    out_specs = [act_spec, act_spec, act_spec, act_spec]
    if moe is not None:
        out_shape = [jax.ShapeDtypeStruct((n, d), F32)] + out_shape
        out_specs = [row_spec] + out_specs
    outs = pl.pallas_call(
        functools.partial(_mix_in_body, moe is not None),
        out_shape=out_shape, grid=(n // tm,), in_specs=in_specs, out_specs=out_specs,
        compiler_params=_params("arbitrary"), name="mix_in",
    )(*args)
    if moe is None:
        return (x,) + tuple(outs)
    return tuple(outs)


def _stick_break_body(q_ref, k_ref, v_ref, og_ref, o_ref, carry_ref, acc_ref):
    tq, dq = q_ref.shape
    nh = dq // LANES
    nsub = tq // LANES
    qi = pl.program_id(1)
    scale = LANES ** -0.5
    r2 = lax.broadcasted_iota(jnp.int32, (LANES, 2 * LANES), 0)
    c2 = lax.broadcasted_iota(jnp.int32, (LANES, 2 * LANES), 1)
    tri_ones = jnp.where((r2 > c2) | (c2 >= LANES), 1.0, 0.0).astype(BF16)

    carry_ref[...] = jnp.zeros_like(carry_ref)
    acc_ref[...] = jnp.zeros_like(acc_ref)

    def key_block(r0, kb, masked):
        rows = tq - r0
        heads = [slice(hd * LANES, (hd + 1) * LANES) for hd in range(nh)]
        start = pl.multiple_of(kb * LANES, LANES)
        if masked:
            rr = lax.broadcasted_iota(jnp.int32, (rows, LANES), 0)
            cc = lax.broadcasted_iota(jnp.int32, (rows, LANES), 1)
            keep = cc < rr
        zs = [lax.dot_general(q_ref[r0:, hs], k_ref[pl.ds(start, LANES), hs],
                              (((1,), (1,)), ((), ())), preferred_element_type=F32) * scale
              for hs in heads]
        lfs, wbs = [], []
        for z in zs:
            lf = -(jnp.maximum(z, 0.0) + jnp.log(1.0 + jnp.exp(-jnp.abs(z))))
            if masked:
                lf = jnp.where(keep, lf, 0.0)
            lfs.append(lf)
            wbs.append(jnp.dot(lf.astype(BF16), tri_ones, preferred_element_type=F32))
        for hd, hs in enumerate(heads):
            carry = carry_ref[hd, r0:, :]
            a = jnp.exp(lfs[hd] + zs[hd] + wbs[hd][:, :LANES] + carry)
            if masked:
                a = jnp.where(keep, a, 0.0)
            acc_ref[hd, r0:, :] += jnp.dot(a.astype(BF16), v_ref[pl.ds(start, LANES), hs],
                                           preferred_element_type=F32)
            carry_ref[hd, r0:, :] = carry + wbs[hd][:, LANES:]

    for j in reversed(range(nsub)):
        key_block(j * LANES, qi * nsub + j, True)

    def more(state):
        kb, live = state
        return jnp.logical_and(kb >= 0, live)

    def step(state):
        kb, _ = state
        key_block(0, kb, False)
        return kb - 1, jnp.max(carry_ref[...]) > LOG_WEIGHT_FLOOR

    lax.while_loop(more, step, (qi * nsub - 1, jnp.max(carry_ref[...]) > LOG_WEIGHT_FLOOR))
    for hd in range(nh):
        hs = slice(hd * LANES, (hd + 1) * LANES)
        o_ref[:, hs] = (_rms(acc_ref[hd]) * og_ref[:, hs]).astype(BF16)


def _stick_break(q, k, v, out_g, *, batch, seq, tq):
    n, dq = q.shape
    nh = dq // LANES
    per_b = seq // tq
    q_spec = pl.BlockSpec((tq, dq), lambda b, i: (b * per_b + i, 0))
    kv_spec = pl.BlockSpec((seq, dq), lambda b, i: (b, 0), pipeline_mode=pl.Buffered(1))
    return pl.pallas_call(
        _stick_break_body,
        out_shape=jax.ShapeDtypeStruct((n, dq), BF16),
        grid=(batch, per_b),
        in_specs=[q_spec, kv_spec, kv_spec, pl.BlockSpec((1, dq), lambda b, i: (0, 1))],
        out_specs=q_spec,
        scratch_shapes=[pltpu.VMEM((nh, tq, LANES), F32), pltpu.VMEM((nh, tq, LANES), F32)],
        compiler_params=_params("arbitrary", "arbitrary"),
        name="stick_break",
    )(q, k, v, out_g.reshape(1, -1))


def _mix_out_body(x_ref, ya_ref, yb_ref, mod_ref, woa_ref, wob_ref, g2_ref, wr_ref, br_ref,
                  x1_ref, h2_ref, ri_ref, rw_ref, cnt_ref, count_scr):
    tm = x_ref.shape[0]

    @pl.when(pl.program_id(0) == 0)
    def _():
        count_scr[...] = jnp.zeros_like(count_scr)

    gate1, shift2, scale2 = mod_ref[2], mod_ref[3], mod_ref[4]
    y = (jnp.dot(ya_ref[...], woa_ref[...], preferred_element_type=F32)
         + jnp.dot(yb_ref[...], wob_ref[...], preferred_element_type=F32))
    x1 = x_ref[...] + gate1 * y
    x1_ref[...] = x1
    h2 = _rms(x1) * g2_ref[...] * (1.0 + scale2) + shift2
    _store_token_major(h2_ref, 0, _pack_halves(h2))

    h_hi = h2.astype(BF16)
    h_lo = (h2 - h_hi.astype(F32)).astype(BF16)
    p = jnp.dot(h_hi, wr_ref[...], preferred_element_type=F32)
    lg = (p[:, :LANES] + p[:, LANES:]
          + jnp.dot(h_lo, wr_ref[:, :LANES], preferred_element_type=F32) + br_ref[...])
    lane = lax.broadcasted_iota(jnp.int32, (tm, LANES), 1)
    lane_f = lane.astype(F32)
    neg = -jnp.inf

    def first_max(vals):
        m = jnp.max(vals, axis=-1, keepdims=True)
        idx = jnp.min(jnp.where(vals == m, lane_f, float(LANES)), axis=-1, keepdims=True)
        return m, idx.astype(jnp.int32)

    is_group = lane < N_GROUPS
    gmax, gidx = first_max(jnp.where(is_group, lg, neg))
    p_group = 1.0 / jnp.sum(jnp.where(is_group, jnp.exp(lg - gmax), 0.0), axis=-1, keepdims=True)
    lo = N_GROUPS + gidx * EXPERTS_PER_GROUP
    el = jnp.where((lane >= lo) & (lane < lo + EXPERTS_PER_GROUP), lg, neg)
    m1, i1 = first_max(el)
    m2, i2 = first_max(jnp.where(lane == i1, neg, el))
    t = jnp.exp(m2 - m1)
    w1 = p_group / (1.0 + t)
    w2 = w1 * t
    e1 = i1 - N_GROUPS
    e2 = i2 - N_GROUPS

    hit1 = lane == e1
    hit2 = lane == e2
    onehot = jnp.where(hit1 | hit2, 1.0, 0.0)
    rr = lax.broadcasted_iota(jnp.int32, (tm, tm), 0)
    cc = lax.broadcasted_iota(jnp.int32, (tm, tm), 1)
    earlier = jnp.where(rr > cc, 1.0, 0.0).astype(BF16)
    base = jnp.dot(earlier, onehot.astype(BF16), preferred_element_type=F32) + count_scr[...]
    r1 = jnp.sum(jnp.where(hit1, base, 0.0), axis=-1, keepdims=True).astype(jnp.int32)
    r2 = jnp.sum(jnp.where(hit2, base, 0.0), axis=-1, keepdims=True).astype(jnp.int32)
    count_scr[...] += jnp.sum(onehot, axis=0, keepdims=True)
    cnt_ref[...] = count_scr[...].astype(jnp.int32)

    l8 = lax.broadcasted_iota(jnp.int32, ri_ref.shape, 1)
    ri_ref[...] = jnp.where(l8 == 0, e1, jnp.where(l8 == 1, e2, jnp.where(l8 == 2, r1, r2)))
    rw_ref[...] = jnp.where(l8 == 0, w1, w2)


def _mix_out(x, ya, yb, mod, w_out, g2, w_router, b_router, *, seq, tm):
    n, d = x.shape
    dq = ya.shape[1]
    nch = d // 2 // LANES
    per_b = seq // tm
    row_spec = pl.BlockSpec((tm, d), lambda i: (i, 0))
    act_spec = pl.BlockSpec((tm, dq), lambda i: (i, 0))
    small_spec = pl.BlockSpec((tm, 8), lambda i: (i, 0))
    const2 = lambda i: (0, 0)
    return pl.pallas_call(
        _mix_out_body,
        out_shape=[jax.ShapeDtypeStruct((n, d), F32),
                   jax.ShapeDtypeStruct((n * nch, LANES), jnp.uint32),
                   jax.ShapeDtypeStruct((n, 8), jnp.int32), jax.ShapeDtypeStruct((n, 8), F32),
                   jax.ShapeDtypeStruct((1, LANES), jnp.int32)],
        grid=(n // tm,),
        in_specs=[row_spec, act_spec, act_spec,
                  pl.BlockSpec((None, N_MOD, 1, d), lambda i: (i // per_b, 0, 0, 0)),
                  pl.BlockSpec((dq, d), const2),
                  pl.BlockSpec((dq, d), lambda i: (1, 0)),
                  pl.BlockSpec((1, d), const2),
                  pl.BlockSpec((d, 2 * LANES), const2),
                  pl.BlockSpec((1, LANES), const2)],
        out_specs=[row_spec, pl.BlockSpec((tm * nch, LANES), lambda i: (i, 0)), small_spec, small_spec,
                   pl.BlockSpec((1, LANES), const2)],
        scratch_shapes=[pltpu.VMEM((1, LANES), F32)],
        compiler_params=_params("arbitrary"),
        name="mix_out",
    )(x, ya, yb, mod, w_out, w_out, g2.reshape(1, d), w_router, b_router)


def _experts_body(n_tok, slot_ref, be_ref, nv_ref, nu_ref, h_hbm, zeros_hbm, wg_ref, wu_ref, wd_ref,
                  out_hbm, code_ref, xbuf, ybuf, wg_b, wu_b, wd_b, gsem, ssem, zsem):
    nch = wg_b.shape[0] // 2 // LANES
    te = ybuf.shape[0] // nch
    i = pl.program_id(0)
    n_used = nu_ref[0]
    n_assign = slot_ref.shape[0]

    def token_of(code):
        if n_tok & (n_tok - 1) == 0:
            return code & (n_tok - 1)
        return jnp.where(code >= n_tok, code - n_tok, code)

    def token_rows(ref, index):
        start = index * nch
        if not isinstance(index, int):
            start = pl.multiple_of(start, nch)
        return ref.at[pl.ds(start, nch), :]

    def start_gather(blk, buf):
        base = blk * te
        for r in range(te):
            tok = token_of(code_ref[base + r])
            pltpu.make_async_copy(token_rows(h_hbm, tok), token_rows(xbuf, buf * te + r),
                                  gsem.at[buf]).start(priority=r % 2)

    def wait_scatter(rows):
        for b in reversed(range(te.bit_length())):
            size = (1 << b) * nch

            @pl.when((rows & (1 << b)) != 0)
            def _():
                pltpu.make_async_copy(ybuf.at[pl.ds(0, size), :], out_hbm.at[pl.ds(0, size), :],
                                      ssem).wait()

    @pl.when(i == 0)
    def _():
        fill = pltpu.make_async_copy(zeros_hbm, code_ref, zsem)
        fill.start()
        fill.wait()

        def invert(j, carry):
            for u in range(ROW_GROUP):
                a = j * ROW_GROUP + u
                code_ref[slot_ref[a]] = a
            return carry

        lax.fori_loop(0, n_assign // ROW_GROUP, invert, 0)
        start_gather(0, 0)

    for buf in range(2):
        @pl.when(jnp.logical_and(i + 1 < n_used, (i + 1) % 2 == buf))
        def _():
            start_gather(i + 1, buf)

    @pl.when(i < n_used)
    def _():
        base = i * te
        buf = i % 2
        n_valid = nv_ref[i]
        @pl.when(jnp.logical_or(i == 0, be_ref[i] != be_ref[jnp.maximum(i - 1, 0)]))
        def _():
            wg_b[...] = wg_ref[...].astype(BF16)
            wu_b[...] = wu_ref[...].astype(BF16)
            wd_b[...] = wd_ref[...].astype(BF16)

        xrows = pl.multiple_of(buf * (te * nch), te * nch)
        pltpu.make_async_copy(h_hbm.at[pl.ds(0, te * nch), :], xbuf.at[pl.ds(xrows, te * nch), :],
                              gsem.at[buf]).wait()
        half = nch * LANES
        x_lo, x_hi = _unpack_halves(_load_token_major(xbuf, xrows, te, nch))
        x_lo, x_hi = x_lo.astype(BF16), x_hi.astype(BF16)
        g = (jnp.dot(x_lo, wg_b[:half], preferred_element_type=F32)
             + jnp.dot(x_hi, wg_b[half:], preferred_element_type=F32))
        u = (jnp.dot(x_lo, wu_b[:half], preferred_element_type=F32)
             + jnp.dot(x_hi, wu_b[half:], preferred_element_type=F32))
        act = (g * jax.nn.sigmoid(g) * u).astype(BF16)

        @pl.when(i > 0)
        def _():
            wait_scatter(nv_ref[jnp.maximum(i - 1, 0)])

        _store_token_major(ybuf, 0, _pack_halves(jnp.dot(act, wd_b[...],
                                                          preferred_element_type=F32)))

        def scatter(r, priority):
            pltpu.make_async_copy(token_rows(ybuf, r), token_rows(out_hbm, code_ref[base + r]),
                                  ssem).start(priority=priority)

        n_groups = lax.shift_right_logical(n_valid, ROW_GROUP.bit_length() - 1)
        for j in range(te // ROW_GROUP):
            @pl.when(j < n_groups)
            def _():
                for u in range(ROW_GROUP):
                    scatter(j * ROW_GROUP + u, u % 2)

        def scatter_one(r, carry):
            scatter(r, 0)
            return carry

        lax.fori_loop(n_groups * ROW_GROUP, n_valid, scatter_one, 0)

        @pl.when(i == n_used - 1)
        def _():
            wait_scatter(n_valid)


def _experts(h2, slot, block_e, n_valid, n_used, w_gate, w_up, w_down, *, layer, te):
    d, de = w_gate.shape[2:]
    nch = d // 2 // LANES
    n = h2.shape[0] // nch
    assert nch % 8 == 0, "a token must cover whole (8, 128) tiles"
    n_blocks = block_e.shape[0]
    cap = n_blocks * te
    assert slot.shape[0] % ROW_GROUP == 0 and te % ROW_GROUP == 0
    by_expert = lambda i, slot, be, nv, nu: (layer, be[i], 0, 0)
    grid_spec = pltpu.PrefetchScalarGridSpec(
        num_scalar_prefetch=4,
        grid=(n_blocks,),
        in_specs=[pl.BlockSpec(memory_space=pl.ANY),
                  pl.BlockSpec(memory_space=pl.ANY),
                  pl.BlockSpec((None, None, d, de), by_expert),
                  pl.BlockSpec((None, None, d, de), by_expert),
                  pl.BlockSpec((None, None, de, d), by_expert)],
        out_specs=pl.BlockSpec(memory_space=pl.ANY),
        scratch_shapes=[pltpu.SMEM((cap,), jnp.int32),
                        pltpu.VMEM((2 * te * nch, LANES), jnp.uint32),
                        pltpu.VMEM((te * nch, LANES), jnp.uint32),
                        pltpu.VMEM((d, de), BF16), pltpu.VMEM((d, de), BF16),
                        pltpu.VMEM((de, d), BF16),
                        pltpu.SemaphoreType.DMA((2,)), pltpu.SemaphoreType.DMA(()),
                        pltpu.SemaphoreType.DMA(())],
    )
    return pl.pallas_call(
        functools.partial(_experts_body, n),
        out_shape=jax.ShapeDtypeStruct((TOP_K * n * nch, LANES), jnp.uint32),
        grid_spec=grid_spec,
        compiler_params=_params("arbitrary"),
        name="moe_experts",
    )(slot, block_e, n_valid, n_used, h2, jnp.zeros((cap,), jnp.int32), w_gate, w_up, w_down)


def _dispatch(ri, counts, *, te):
    n = ri.shape[0]
    counts = counts[0, :N_EXPERTS]
    padded = (counts + te - 1) // te * te
    pend = jnp.cumsum(padded)
    pstart = pend - padded
    n_blocks = TOP_K * n // te + N_EXPERTS
    slot = (pstart[ri[:, :TOP_K]] + ri[:, TOP_K:2 * TOP_K]).T.reshape(-1)
    block_start = jnp.arange(n_blocks, dtype=jnp.int32) * te
    block_e = jnp.minimum(jnp.sum(pend[None, :] <= block_start[:, None], axis=1),
                          N_EXPERTS - 1).astype(jnp.int32)
    n_valid = jnp.clip(pstart[block_e] + counts[block_e] - block_start, 0, te).astype(jnp.int32)
    n_used = (pend[-1:] // te).astype(jnp.int32)
    return slot.astype(jnp.int32), block_e, n_valid, n_used


def _combine_body(x_ref, m0_ref, m1_ref, rw_ref, g2_ref, o_ref):
    o_ref[...] = _moe_residual(x_ref[...], m0_ref, m1_ref, rw_ref, g2_ref)


def _combine(x, m, rw, gate2, *, seq, tm):
    n, d = x.shape
    nch = d // 2 // LANES
    per_b = seq // tm
    row_spec = pl.BlockSpec((tm, d), lambda i: (i, 0))
    return pl.pallas_call(
        _combine_body,
        out_shape=jax.ShapeDtypeStruct((n, d), F32),
        grid=(n // tm,),
        in_specs=[row_spec,
                  pl.BlockSpec((tm * nch, LANES), lambda i: (i, 0)),
                  pl.BlockSpec((tm * nch, LANES), lambda i: (n // tm + i, 0)),
                  pl.BlockSpec((tm, rw.shape[1]), lambda i: (i, 0)),
                  pl.BlockSpec((None, 1, d), lambda i: (i // per_b, 0, 0))],
        out_specs=row_spec,
        compiler_params=_params("arbitrary"),
        name="moe_combine",
    )(x, m, m, rw, gate2)


def _tiles(seq):
    tm = min(512, seq)
    tq = min(256, seq)
    te = 256
    return tm, tq, te


def kernel(x, c, w_mod, b_mod, mod_layer, norm1_g, w_in, gm_norm_g, gm_ws, gm_bs, q_norm_g, k_norm_g, out_norm_g, w_out, norm2_g, w_group, b_group, w_route, b_route, w_gate, w_up, w_down):
    batch, seq, d = x.shape
    depth = mod_layer.shape[0]
    n = batch * seq
    tm, tq, te = _tiles(seq)
    assert seq % tm == 0 and seq % tq == 0 and tm % LANES == 0 and tq % LANES == 0
    assert (TOP_K * n) % te == 0 and w_group.shape[2] == N_GROUPS and w_route.shape[2] == N_EXPERTS
    assert w_in.shape[2] * 2 == 5 * w_out.shape[1]

    mod_all = _modulation(c, w_mod, b_mod, mod_layer).reshape(depth, batch, N_MOD, 1, d)
    w_in_b, w_out_b = w_in.astype(BF16), w_out.astype(BF16)
    pad = LANES - N_GROUPS - N_EXPERTS
    w_router = jnp.pad(jnp.concatenate([w_group, w_route], axis=2), ((0, 0), (0, 0), (0, pad)))
    w_router_hi = w_router.astype(BF16)
    w_router_lo = (w_router - w_router_hi.astype(F32)).astype(BF16)
    w_router = jnp.concatenate([w_router_hi, w_router_lo], axis=2)
    b_router = jnp.pad(jnp.concatenate([b_group, b_route], axis=1), ((0, 0), (0, pad)))

    xf = x.reshape(n, d)
    moe = None
    for l in range(depth):
        xf, ya, q, k, v = _mix_in(xf, moe, mod_all[l], norm1_g[l], w_in_b[l], gm_norm_g[l], gm_ws[l],
                                  gm_bs[l], q_norm_g[l], k_norm_g[l], out_norm_g[l], seq=seq, tm=tm)
        yb = _stick_break(q, k, v, out_norm_g[l], batch=batch, seq=seq, tq=tq)
        xf, h2, ri, rw, counts = _mix_out(xf, ya, yb, mod_all[l], w_out_b[l], norm2_g[l],
                                          w_router[l], b_router[l:l + 1], seq=seq, tm=tm)
        slot, block_e, n_valid, n_used = _dispatch(ri, counts, te=te)
        m = _experts(h2, slot, block_e, n_valid, n_used, w_gate, w_up, w_down, layer=l, te=te)
        moe = (m, rw, mod_all[l][:, 5])
    out = _combine(xf, *moe, seq=seq, tm=tm)
    return out.reshape(batch, seq, d)
```

```python
import functools

import jax
import jax.numpy as jnp
from jax import lax
from jax.experimental import pallas as pl
from jax.experimental.pallas import tpu as pltpu

F32 = jnp.float32
BF16 = jnp.bfloat16

EPS = 1e-6
LANES = 128
N_MOD = 6
N_GROUPS = 4
EXPERTS_PER_GROUP = 8
N_EXPERTS = N_GROUPS * EXPERTS_PER_GROUP
TOP_K = 2
LOG_WEIGHT_FLOOR = -104.0
ROW_GROUP = 8
VMEM_LIMIT = 56 * 1024 * 1024
HIGH_HALF = 0xFFFF0000


def _rms(x):
    return x * lax.rsqrt(jnp.mean(x * x, axis=-1, keepdims=True) + EPS)


def _gelu(x):
    return 0.5 * x * (1.0 + lax.erf(x * (2.0 ** -0.5)))


def _pack_halves(x):
    half = x.shape[1] // 2
    bits = lax.bitcast_convert_type(x.astype(BF16).astype(F32), jnp.uint32)
    return (lax.shift_right_logical(bits[:, :half], jnp.uint32(16))
            | (bits[:, half:] & jnp.uint32(HIGH_HALF)))


def _unpack_halves(w):
    lo = lax.bitcast_convert_type(lax.shift_left(w, jnp.uint32(16)), F32)
    hi = lax.bitcast_convert_type(w & jnp.uint32(HIGH_HALF), F32)
    return lo, hi


def _store_token_major(ref, base, packed):
    rows, width = packed.shape
    nch = width // LANES
    for c in range(nch):
        ref[pl.ds(base + c, rows, stride=nch), :] = packed[:, c * LANES:(c + 1) * LANES]


def _load_token_major(ref, base, rows, nch):
    return jnp.concatenate([ref[pl.ds(base + c, rows, stride=nch), :] for c in range(nch)], axis=-1)


def _moe_residual(x, slot_ref, m_hbm, mbuf, msem, rw_ref, g2_ref):
    tm, d = x.shape
    nch = d // 2 // LANES
    i = pl.program_id(0)
    n_tok = slot_ref.shape[0] // TOP_K
    buf_rows = TOP_K * tm * nch

    def start_gather(step, buf):
        def group(j, carry):
            for u in range(ROW_GROUP):
                r = j * ROW_GROUP + u
                for k in range(TOP_K):
                    src = pl.multiple_of(slot_ref[k * n_tok + step * tm + r] * nch, nch)
                    dst = pl.multiple_of(buf * buf_rows + (k * tm + r) * nch, nch)
                    pltpu.make_async_copy(m_hbm.at[pl.ds(src, nch), :], mbuf.at[pl.ds(dst, nch), :],
                                          msem.at[buf]).start(priority=k)
            return carry

        lax.fori_loop(0, tm // ROW_GROUP, group, 0)

    @pl.when(i == 0)
    def _():
        start_gather(0, 0)

    @pl.when(i + 1 < pl.num_programs(0))
    def _():
        start_gather(i + 1, (i + 1) % 2)

    buf = i % 2
    base = pl.multiple_of(buf * buf_rows, buf_rows)
    pltpu.make_async_copy(m_hbm.at[pl.ds(0, buf_rows), :], mbuf.at[pl.ds(base, buf_rows), :],
                          msem.at[buf]).wait()
    lo0, hi0 = _unpack_halves(_load_token_major(mbuf, base, tm, nch))
    lo1, hi1 = _unpack_halves(_load_token_major(mbuf, base + tm * nch, tm, nch))
    w1, w2 = rw_ref[:, 0:1], rw_ref[:, 1:2]
    m = jnp.concatenate([w1 * lo0 + w2 * lo1, w1 * hi0 + w2 * hi1], axis=-1)
    return x + g2_ref[...] * m


def _moe_residual_scratch(tm, d):
    nch = d // 2 // LANES
    return [pltpu.VMEM((2 * TOP_K * tm * nch, LANES), jnp.uint32), pltpu.SemaphoreType.DMA((2,))]


def _params(*sem):
    return pltpu.CompilerParams(dimension_semantics=sem, vmem_limit_bytes=VMEM_LIMIT)


def _mod_body(c_ref, w_ref, b_ref, ml_ref, o_ref):
    c = c_ref[...]
    sc = c * jax.nn.sigmoid(c)
    r = jnp.dot(sc, w_ref[...], preferred_element_type=F32,
                precision=lax.Precision.HIGHEST) + b_ref[...]
    for l in range(o_ref.shape[0]):
        o_ref[l] = r + ml_ref[l:l + 1, :]


def _modulation(c, w_mod, b_mod, mod_layer):
    b, d = c.shape
    depth, w = mod_layer.shape
    tn = 1024
    return pl.pallas_call(
        _mod_body,
        out_shape=jax.ShapeDtypeStruct((depth, b, w), F32),
        grid=(w // tn,),
        in_specs=[pl.BlockSpec((b, d), lambda j: (0, 0)),
                  pl.BlockSpec((d, tn), lambda j: (0, j)),
                  pl.BlockSpec((1, tn), lambda j: (0, j)),
                  pl.BlockSpec((depth, tn), lambda j: (0, j))],
        out_specs=pl.BlockSpec((depth, b, tn), lambda j: (0, 0, j)),
        compiler_params=_params("arbitrary"),
        name="modulation",
    )(c, w_mod, b_mod.reshape(1, w), mod_layer)


def _mix_in_body(has_moe, *refs):
    refs = list(refs)
    if has_moe:
        slot_ref = refs.pop(0)
    x_ref = refs.pop(0)
    if has_moe:
        m_hbm, rw_ref, g2_ref = refs.pop(0), refs.pop(0), refs.pop(0)
    (mod_ref, g1_ref, win_ref, gmg_ref, ws_ref, bs_ref, qg_ref, kg_ref, og_ref) = refs[:9]
    refs = refs[9:]
    if has_moe:
        xo_ref = refs.pop(0)
    ya_ref, q_ref, k_ref, v_ref = refs[:4]

    x = x_ref[...]
    if has_moe:
        mbuf, msem = refs[4:]
        x = _moe_residual(x, slot_ref, m_hbm, mbuf, msem, rw_ref, g2_ref)
        xo_ref[...] = x
    tm = x.shape[0]
    dq = ya_ref.shape[1]
    shift1, scale1 = mod_ref[0], mod_ref[1]
    h = _rms(x) * g1_ref[...] * (1.0 + scale1) + shift1
    proj = jnp.dot(h.astype(BF16), win_ref[...], preferred_element_type=F32)

    row = lax.broadcasted_iota(jnp.int32, (LANES, LANES), 0)
    col = lax.broadcasted_iota(jnp.int32, (LANES, LANES), 1)
    causal = row >= col
    for g in range(dq // LANES):
        sl = slice(g * LANES, (g + 1) * LANES)
        u = _gelu(proj[:, g * LANES:(g + 1) * LANES])
        vg = _gelu(proj[:, dq + g * LANES:dq + (g + 1) * LANES])
        vg = (_rms(vg) * gmg_ref[:, sl]).astype(BF16)
        wg = jnp.where(causal, ws_ref[g], 0.0).astype(BF16)
        for c in range(tm // LANES):
            rs = slice(c * LANES, (c + 1) * LANES)
            s = jnp.dot(wg, vg[rs], preferred_element_type=F32) + bs_ref[g]
            ya = u[rs] * s
            ya_ref[rs, sl] = (_rms(ya) * og_ref[:, sl]).astype(BF16)
    for hd in range(dq // LANES):
        sl = slice(hd * LANES, (hd + 1) * LANES)
        qh = proj[:, 2 * dq + hd * LANES:2 * dq + (hd + 1) * LANES]
        kh = proj[:, 3 * dq + hd * LANES:3 * dq + (hd + 1) * LANES]
        q_ref[:, sl] = (_rms(qh) * qg_ref[...]).astype(BF16)
        k_ref[:, sl] = (_rms(kh) * kg_ref[...]).astype(BF16)
    v_ref[...] = proj[:, 4 * dq:].astype(BF16)


def _mix_in(x, moe, mod, g1, w_in, gm_g, gm_ws, gm_bs, q_g, k_g, out_g, *, seq, tm):
    n, d = x.shape
    dq = w_in.shape[1] // 5
    ng = dq // LANES
    per_b = seq // tm
    row_spec = pl.BlockSpec((tm, d), lambda i, *_: (i, 0))
    const2 = lambda i, *_: (0, 0)
    const3 = lambda i, *_: (0, 0, 0)
    in_specs = [row_spec]
    args = [x]
    prefetch = []
    scratch = []
    if moe is not None:
        slot, m, rw, gate2 = moe
        prefetch = [slot]
        in_specs += [pl.BlockSpec(memory_space=pl.ANY),
                     pl.BlockSpec((tm, rw.shape[1]), lambda i, *_: (i, 0)),
                     pl.BlockSpec((None, 1, d), lambda i, *_: (i // per_b, 0, 0))]
        args += [m, rw, gate2]
        scratch = _moe_residual_scratch(tm, d)
    in_specs += [pl.BlockSpec((None, N_MOD, 1, d), lambda i, *_: (i // per_b, 0, 0, 0)),
                 pl.BlockSpec((1, d), const2),
                 pl.BlockSpec(w_in.shape, const2),
                 pl.BlockSpec((1, dq), const2),
                 pl.BlockSpec((ng, LANES, LANES), const3),
                 pl.BlockSpec((ng, LANES, 1), const3),
                 pl.BlockSpec((1, LANES), const2),
                 pl.BlockSpec((1, LANES), const2),
                 pl.BlockSpec((1, dq), const2)]
    args += [mod, g1.reshape(1, d), w_in, gm_g.reshape(1, dq), gm_ws, gm_bs.reshape(ng, LANES, 1),
             q_g.reshape(1, LANES), k_g.reshape(1, LANES), out_g.reshape(1, -1)]
    act = jax.ShapeDtypeStruct((n, dq), BF16)
    act_spec = pl.BlockSpec((tm, dq), lambda i, *_: (i, 0))
    out_shape = [act, act, act, act]
    out_specs = [act_spec, act_spec, act_spec, act_spec]
    if moe is not None:
        out_shape = [jax.ShapeDtypeStruct((n, d), F32)] + out_shape
        out_specs = [row_spec] + out_specs
    outs = pl.pallas_call(
        functools.partial(_mix_in_body, moe is not None),
        out_shape=out_shape,
        grid_spec=pltpu.PrefetchScalarGridSpec(
            num_scalar_prefetch=len(prefetch), grid=(n // tm,), in_specs=in_specs,
            out_specs=out_specs, scratch_shapes=scratch),
        compiler_params=_params("arbitrary"), name="mix_in",
    )(*prefetch, *args)
    if moe is None:
        return (x,) + tuple(outs)
    return tuple(outs)


def _stick_break_body(q_ref, k_ref, v_ref, og_ref, o_ref, carry_ref, acc_ref):
    tq, dq = q_ref.shape
    nh = dq // LANES
    nsub = tq // LANES
    qi = pl.program_id(1)
    scale = LANES ** -0.5
    r2 = lax.broadcasted_iota(jnp.int32, (LANES, 2 * LANES), 0)
    c2 = lax.broadcasted_iota(jnp.int32, (LANES, 2 * LANES), 1)
    tri_ones = jnp.where((r2 > c2) | (c2 >= LANES), 1.0, 0.0).astype(BF16)

    carry_ref[...] = jnp.zeros_like(carry_ref)
    acc_ref[...] = jnp.zeros_like(acc_ref)

    def key_block(r0, kb, masked):
        rows = tq - r0
        heads = [slice(hd * LANES, (hd + 1) * LANES) for hd in range(nh)]
        start = pl.multiple_of(kb * LANES, LANES)
        if masked:
            rr = lax.broadcasted_iota(jnp.int32, (rows, LANES), 0)
            cc = lax.broadcasted_iota(jnp.int32, (rows, LANES), 1)
            keep = cc < rr
        zs = [lax.dot_general(q_ref[r0:, hs], k_ref[pl.ds(start, LANES), hs],
                              (((1,), (1,)), ((), ())), preferred_element_type=F32) * scale
              for hs in heads]
        lfs, wbs = [], []
        for z in zs:
            lf = -(jnp.maximum(z, 0.0) + jnp.log(1.0 + jnp.exp(-jnp.abs(z))))
            if masked:
                lf = jnp.where(keep, lf, 0.0)
            lfs.append(lf)
            wbs.append(jnp.dot(lf.astype(BF16), tri_ones, preferred_element_type=F32))
        for hd, hs in enumerate(heads):
            carry = carry_ref[hd, r0:, :]
            a = jnp.exp(lfs[hd] + zs[hd] + wbs[hd][:, :LANES] + carry)
            if masked:
                a = jnp.where(keep, a, 0.0)
            acc_ref[hd, r0:, :] += jnp.dot(a.astype(BF16), v_ref[pl.ds(start, LANES), hs],
                                           preferred_element_type=F32)
            carry_ref[hd, r0:, :] = carry + wbs[hd][:, LANES:]

    for j in reversed(range(nsub)):
        key_block(j * LANES, qi * nsub + j, True)

    def more(state):
        kb, live = state
        return jnp.logical_and(kb >= 0, live)

    def step(state):
        kb, _ = state
        key_block(0, kb, False)
        return kb - 1, jnp.max(carry_ref[...]) > LOG_WEIGHT_FLOOR

    lax.while_loop(more, step, (qi * nsub - 1, jnp.max(carry_ref[...]) > LOG_WEIGHT_FLOOR))
    for hd in range(nh):
        hs = slice(hd * LANES, (hd + 1) * LANES)
        o_ref[:, hs] = (_rms(acc_ref[hd]) * og_ref[:, hs]).astype(BF16)


def _stick_break(q, k, v, out_g, *, batch, seq, tq):
    n, dq = q.shape
    nh = dq // LANES
    per_b = seq // tq
    q_spec = pl.BlockSpec((tq, dq), lambda b, i: (b * per_b + i, 0))
    kv_spec = pl.BlockSpec((seq, dq), lambda b, i: (b, 0), pipeline_mode=pl.Buffered(1))
    return pl.pallas_call(
        _stick_break_body,
        out_shape=jax.ShapeDtypeStruct((n, dq), BF16),
        grid=(batch, per_b),
        in_specs=[q_spec, kv_spec, kv_spec, pl.BlockSpec((1, dq), lambda b, i: (0, 1))],
        out_specs=q_spec,
        scratch_shapes=[pltpu.VMEM((nh, tq, LANES), F32), pltpu.VMEM((nh, tq, LANES), F32)],
        compiler_params=_params("arbitrary", "arbitrary"),
        name="stick_break",
    )(q, k, v, out_g.reshape(1, -1))


def _mix_out_body(x_ref, ya_ref, yb_ref, mod_ref, woa_ref, wob_ref, g2_ref, wr_ref, br_ref,
                  x1_ref, h2_ref, ri_ref, rw_ref, cnt_ref, count_scr):
    tm = x_ref.shape[0]

    @pl.when(pl.program_id(0) == 0)
    def _():
        count_scr[...] = jnp.zeros_like(count_scr)

    gate1, shift2, scale2 = mod_ref[2], mod_ref[3], mod_ref[4]
    y = (jnp.dot(ya_ref[...], woa_ref[...], preferred_element_type=F32)
         + jnp.dot(yb_ref[...], wob_ref[...], preferred_element_type=F32))
    x1 = x_ref[...] + gate1 * y
    x1_ref[...] = x1
    h2 = _rms(x1) * g2_ref[...] * (1.0 + scale2) + shift2
    _store_token_major(h2_ref, 0, _pack_halves(h2))

    h_hi = h2.astype(BF16)
    h_lo = (h2 - h_hi.astype(F32)).astype(BF16)
    p = jnp.dot(h_hi, wr_ref[...], preferred_element_type=F32)
    lg = (p[:, :LANES] + p[:, LANES:]
          + jnp.dot(h_lo, wr_ref[:, :LANES], preferred_element_type=F32) + br_ref[...])
    lane = lax.broadcasted_iota(jnp.int32, (tm, LANES), 1)
    lane_f = lane.astype(F32)
    neg = -jnp.inf

    def first_max(vals):
        m = jnp.max(vals, axis=-1, keepdims=True)
        idx = jnp.min(jnp.where(vals == m, lane_f, float(LANES)), axis=-1, keepdims=True)
        return m, idx.astype(jnp.int32)

    is_group = lane < N_GROUPS
    gmax, gidx = first_max(jnp.where(is_group, lg, neg))
    p_group = 1.0 / jnp.sum(jnp.where(is_group, jnp.exp(lg - gmax), 0.0), axis=-1, keepdims=True)
    lo = N_GROUPS + gidx * EXPERTS_PER_GROUP
    el = jnp.where((lane >= lo) & (lane < lo + EXPERTS_PER_GROUP), lg, neg)
    m1, i1 = first_max(el)
    m2, i2 = first_max(jnp.where(lane == i1, neg, el))
    t = jnp.exp(m2 - m1)
    w1 = p_group / (1.0 + t)
    w2 = w1 * t
    e1 = i1 - N_GROUPS
    e2 = i2 - N_GROUPS

    hit1 = lane == e1
    hit2 = lane == e2
    onehot = jnp.where(hit1 | hit2, 1.0, 0.0)
    rr = lax.broadcasted_iota(jnp.int32, (tm, tm), 0)
    cc = lax.broadcasted_iota(jnp.int32, (tm, tm), 1)
    earlier = jnp.where(rr > cc, 1.0, 0.0).astype(BF16)
    base = jnp.dot(earlier, onehot.astype(BF16), preferred_element_type=F32) + count_scr[...]
    r1 = jnp.sum(jnp.where(hit1, base, 0.0), axis=-1, keepdims=True).astype(jnp.int32)
    r2 = jnp.sum(jnp.where(hit2, base, 0.0), axis=-1, keepdims=True).astype(jnp.int32)
    count_scr[...] += jnp.sum(onehot, axis=0, keepdims=True)
    cnt_ref[...] = count_scr[...].astype(jnp.int32)

    l8 = lax.broadcasted_iota(jnp.int32, ri_ref.shape, 1)
    ri_ref[...] = jnp.where(l8 == 0, e1, jnp.where(l8 == 1, e2, jnp.where(l8 == 2, r1, r2)))
    rw_ref[...] = jnp.where(l8 == 0, w1, w2)


def _mix_out(x, ya, yb, mod, w_out, g2, w_router, b_router, *, seq, tm):
    n, d = x.shape
    dq = ya.shape[1]
    nch = d // 2 // LANES
    per_b = seq // tm
    row_spec = pl.BlockSpec((tm, d), lambda i: (i, 0))
    act_spec = pl.BlockSpec((tm, dq), lambda i: (i, 0))
    small_spec = pl.BlockSpec((tm, 8), lambda i: (i, 0))
    const2 = lambda i: (0, 0)
    return pl.pallas_call(
        _mix_out_body,
        out_shape=[jax.ShapeDtypeStruct((n, d), F32),
                   jax.ShapeDtypeStruct((n * nch, LANES), jnp.uint32),
                   jax.ShapeDtypeStruct((n, 8), jnp.int32), jax.ShapeDtypeStruct((n, 8), F32),
                   jax.ShapeDtypeStruct((1, LANES), jnp.int32)],
        grid=(n // tm,),
        in_specs=[row_spec, act_spec, act_spec,
                  pl.BlockSpec((None, N_MOD, 1, d), lambda i: (i // per_b, 0, 0, 0)),
                  pl.BlockSpec((dq, d), const2),
                  pl.BlockSpec((dq, d), lambda i: (1, 0)),
                  pl.BlockSpec((1, d), const2),
                  pl.BlockSpec((d, 2 * LANES), const2),
                  pl.BlockSpec((1, LANES), const2)],
        out_specs=[row_spec, pl.BlockSpec((tm * nch, LANES), lambda i: (i, 0)), small_spec, small_spec,
                   pl.BlockSpec((1, LANES), const2)],
        scratch_shapes=[pltpu.VMEM((1, LANES), F32)],
        compiler_params=_params("arbitrary"),
        name="mix_out",
    )(x, ya, yb, mod, w_out, w_out, g2.reshape(1, d), w_router, b_router)


def _experts_body(n_tok, slot_ref, be_ref, nu_ref, h_hbm, zeros_hbm, wg_ref, wu_ref, wd_ref,
                  out_ref, code_ref, xbuf, wg_b, wu_b, wd_b, gsem, zsem):
    nch = wg_b.shape[0] // 2 // LANES
    te = out_ref.shape[0] // nch
    i = pl.program_id(0)
    n_used = nu_ref[0]
    n_assign = slot_ref.shape[0]

    def token_of(code):
        if n_tok & (n_tok - 1) == 0:
            return code & (n_tok - 1)
        return jnp.where(code >= n_tok, code - n_tok, code)

    def token_rows(ref, index):
        start = index * nch
        if not isinstance(index, int):
            start = pl.multiple_of(start, nch)
        return ref.at[pl.ds(start, nch), :]

    def start_gather(blk, buf):
        base = blk * te
        for r in range(te):
            tok = token_of(code_ref[base + r])
            pltpu.make_async_copy(token_rows(h_hbm, tok), token_rows(xbuf, buf * te + r),
                                  gsem.at[buf]).start(priority=r % 2)

    @pl.when(i == 0)
    def _():
        fill = pltpu.make_async_copy(zeros_hbm, code_ref, zsem)
        fill.start()
        fill.wait()

        def invert(j, carry):
            for u in range(ROW_GROUP):
                a = j * ROW_GROUP + u
                code_ref[slot_ref[a]] = a
            return carry

        lax.fori_loop(0, n_assign // ROW_GROUP, invert, 0)
        start_gather(0, 0)

    for buf in range(2):
        @pl.when(jnp.logical_and(i + 1 < n_used, (i + 1) % 2 == buf))
        def _():
            start_gather(i + 1, buf)

    @pl.when(i < n_used)
    def _():
        buf = i % 2

        @pl.when(jnp.logical_or(i == 0, be_ref[i] != be_ref[jnp.maximum(i - 1, 0)]))
        def _():
            wg_b[...] = wg_ref[...].astype(BF16)
            wu_b[...] = wu_ref[...].astype(BF16)
            wd_b[...] = wd_ref[...].astype(BF16)

        xrows = pl.multiple_of(buf * (te * nch), te * nch)
        pltpu.make_async_copy(h_hbm.at[pl.ds(0, te * nch), :], xbuf.at[pl.ds(xrows, te * nch), :],
                              gsem.at[buf]).wait()
        half = nch * LANES
        x_lo, x_hi = _unpack_halves(_load_token_major(xbuf, xrows, te, nch))
        x_lo, x_hi = x_lo.astype(BF16), x_hi.astype(BF16)
        g = (jnp.dot(x_lo, wg_b[:half], preferred_element_type=F32)
             + jnp.dot(x_hi, wg_b[half:], preferred_element_type=F32))
        u = (jnp.dot(x_lo, wu_b[:half], preferred_element_type=F32)
             + jnp.dot(x_hi, wu_b[half:], preferred_element_type=F32))
        act = (g * jax.nn.sigmoid(g) * u).astype(BF16)
        _store_token_major(out_ref, 0, _pack_halves(jnp.dot(act, wd_b[...],
                                                             preferred_element_type=F32)))

    @pl.when(i >= n_used)
    def _():
        out_ref[...] = jnp.zeros_like(out_ref)


def _experts(h2, slot, block_e, n_used, w_gate, w_up, w_down, *, layer, te):
    d, de = w_gate.shape[2:]
    nch = d // 2 // LANES
    n = h2.shape[0] // nch
    assert nch % 8 == 0, "a token must cover whole (8, 128) tiles"
    n_blocks = block_e.shape[0]
    cap = n_blocks * te
    assert slot.shape[0] % ROW_GROUP == 0
    by_expert = lambda i, slot, be, nu: (layer, be[i], 0, 0)
    grid_spec = pltpu.PrefetchScalarGridSpec(
        num_scalar_prefetch=3,
        grid=(n_blocks,),
        in_specs=[pl.BlockSpec(memory_space=pl.ANY),
                  pl.BlockSpec(memory_space=pl.ANY),
                  pl.BlockSpec((None, None, d, de), by_expert),
                  pl.BlockSpec((None, None, d, de), by_expert),
                  pl.BlockSpec((None, None, de, d), by_expert)],
        out_specs=pl.BlockSpec((te * nch, LANES), lambda i, slot, be, nu: (i, 0)),
        scratch_shapes=[pltpu.SMEM((cap,), jnp.int32),
                        pltpu.VMEM((2 * te * nch, LANES), jnp.uint32),
                        pltpu.VMEM((d, de), BF16), pltpu.VMEM((d, de), BF16),
                        pltpu.VMEM((de, d), BF16),
                        pltpu.SemaphoreType.DMA((2,)), pltpu.SemaphoreType.DMA(())],
    )
    return pl.pallas_call(
        functools.partial(_experts_body, n),
        out_shape=jax.ShapeDtypeStruct((cap * nch, LANES), jnp.uint32),
        grid_spec=grid_spec,
        compiler_params=_params("arbitrary"),
        name="moe_experts",
    )(slot, block_e, n_used, h2, jnp.zeros((cap,), jnp.int32), w_gate, w_up, w_down)


def _dispatch(ri, counts, *, te):
    n = ri.shape[0]
    counts = counts[0, :N_EXPERTS]
    padded = (counts + te - 1) // te * te
    pend = jnp.cumsum(padded)
    pstart = pend - padded
    n_blocks = TOP_K * n // te + N_EXPERTS
    expert = ri[:, :TOP_K].T[:, :, None]
    first = jnp.sum(jnp.where(expert == jnp.arange(N_EXPERTS), pstart, 0), axis=-1)
    slot = (first + ri[:, TOP_K:2 * TOP_K].T).reshape(-1)
    block_start = jnp.arange(n_blocks, dtype=jnp.int32) * te
    block_e = jnp.minimum(jnp.sum(pend[None, :] <= block_start[:, None], axis=1),
                          N_EXPERTS - 1).astype(jnp.int32)
    n_used = (pend[-1:] // te).astype(jnp.int32)
    return slot.astype(jnp.int32), block_e, n_used


def _combine_body(slot_ref, x_ref, m_hbm, rw_ref, g2_ref, o_ref, mbuf, msem):
    o_ref[...] = _moe_residual(x_ref[...], slot_ref, m_hbm, mbuf, msem, rw_ref, g2_ref)


def _combine(x, slot, m, rw, gate2, *, seq, tm):
    n, d = x.shape
    per_b = seq // tm
    row_spec = pl.BlockSpec((tm, d), lambda i, slot: (i, 0))
    return pl.pallas_call(
        _combine_body,
        out_shape=jax.ShapeDtypeStruct((n, d), F32),
        grid_spec=pltpu.PrefetchScalarGridSpec(
            num_scalar_prefetch=1, grid=(n // tm,),
            in_specs=[row_spec,
                      pl.BlockSpec(memory_space=pl.ANY),
                      pl.BlockSpec((tm, rw.shape[1]), lambda i, slot: (i, 0)),
                      pl.BlockSpec((None, 1, d), lambda i, slot: (i // per_b, 0, 0))],
            out_specs=row_spec, scratch_shapes=_moe_residual_scratch(tm, d)),
        compiler_params=_params("arbitrary"),
        name="moe_combine",
    )(slot, x, m, rw, gate2)


def _tiles(seq):
    tm = min(512, seq)
    tq = min(256, seq)
    te = 256
    return tm, tq, te


def kernel(x, c, w_mod, b_mod, mod_layer, norm1_g, w_in, gm_norm_g, gm_ws, gm_bs, q_norm_g, k_norm_g, out_norm_g, w_out, norm2_g, w_group, b_group, w_route, b_route, w_gate, w_up, w_down):
    batch, seq, d = x.shape
    depth = mod_layer.shape[0]
    n = batch * seq
    tm, tq, te = _tiles(seq)
    assert seq % tm == 0 and seq % tq == 0 and tm % LANES == 0 and tq % LANES == 0
    assert (TOP_K * n) % te == 0 and w_group.shape[2] == N_GROUPS and w_route.shape[2] == N_EXPERTS
    assert w_in.shape[2] * 2 == 5 * w_out.shape[1]

    mod_all = _modulation(c, w_mod, b_mod, mod_layer).reshape(depth, batch, N_MOD, 1, d)
    w_in_b, w_out_b = w_in.astype(BF16), w_out.astype(BF16)
    pad = LANES - N_GROUPS - N_EXPERTS
    w_router = jnp.pad(jnp.concatenate([w_group, w_route], axis=2), ((0, 0), (0, 0), (0, pad)))
    w_router_hi = w_router.astype(BF16)
    w_router_lo = (w_router - w_router_hi.astype(F32)).astype(BF16)
    w_router = jnp.concatenate([w_router_hi, w_router_lo], axis=2)
    b_router = jnp.pad(jnp.concatenate([b_group, b_route], axis=1), ((0, 0), (0, pad)))

    xf = x.reshape(n, d)
    moe = None
    for l in range(depth):
        xf, ya, q, k, v = _mix_in(xf, moe, mod_all[l], norm1_g[l], w_in_b[l], gm_norm_g[l], gm_ws[l],
                                  gm_bs[l], q_norm_g[l], k_norm_g[l], out_norm_g[l], seq=seq, tm=tm)
        yb = _stick_break(q, k, v, out_norm_g[l], batch=batch, seq=seq, tq=tq)
        xf, h2, ri, rw, counts = _mix_out(xf, ya, yb, mod_all[l], w_out_b[l], norm2_g[l],
                                          w_router[l], b_router[l:l + 1], seq=seq, tm=tm)
        slot, block_e, n_used = _dispatch(ri, counts, te=te)
        m = _experts(h2, slot, block_e, n_used, w_gate, w_up, w_down, layer=l, te=te)
        moe = (slot, m, rw, mod_all[l][:, 5])
    out = _combine(xf, *moe, seq=seq, tm=tm)
    return out.reshape(batch, seq, d)
```

```python
import functools

import jax
import jax.numpy as jnp
from jax import lax
from jax.experimental import pallas as pl
from jax.experimental.pallas import tpu as pltpu

F32 = jnp.float32
BF16 = jnp.bfloat16

EPS = 1e-6
LANES = 128
N_MOD = 6
N_GROUPS = 4
EXPERTS_PER_GROUP = 8
N_EXPERTS = N_GROUPS * EXPERTS_PER_GROUP
TOP_K = 2
LOG_WEIGHT_FLOOR = -104.0
ROW_GROUP = 8
VMEM_LIMIT = 56 * 1024 * 1024
HIGH_HALF = 0xFFFF0000


def _rms(x):
    return x * lax.rsqrt(jnp.mean(x * x, axis=-1, keepdims=True) + EPS)


def _gelu(x):
    return 0.5 * x * (1.0 + lax.erf(x * (2.0 ** -0.5)))


def _pack_halves(x):
    half = x.shape[1] // 2
    bits = lax.bitcast_convert_type(x.astype(BF16).astype(F32), jnp.uint32)
    return (lax.shift_right_logical(bits[:, :half], jnp.uint32(16))
            | (bits[:, half:] & jnp.uint32(HIGH_HALF)))


def _unpack_halves(w):
    lo = lax.bitcast_convert_type(lax.shift_left(w, jnp.uint32(16)), F32)
    hi = lax.bitcast_convert_type(w & jnp.uint32(HIGH_HALF), F32)
    return lo, hi


def _store_token_major(ref, base, packed):
    rows, width = packed.shape
    nch = width // LANES
    for c in range(nch):
        ref[pl.ds(base + c, rows, stride=nch), :] = packed[:, c * LANES:(c + 1) * LANES]


def _load_token_major(ref, base, rows, nch):
    return jnp.concatenate([ref[pl.ds(base + c, rows, stride=nch), :] for c in range(nch)], axis=-1)


def _moe_residual(x, m0_ref, m1_ref, rw_ref, g2_ref):
    tm, d = x.shape
    nch = d // 2 // LANES
    lo0, hi0 = _unpack_halves(_load_token_major(m0_ref, 0, tm, nch))
    lo1, hi1 = _unpack_halves(_load_token_major(m1_ref, 0, tm, nch))
    w1, w2 = rw_ref[:, 0:1], rw_ref[:, 1:2]
    m = jnp.concatenate([w1 * lo0 + w2 * lo1, w1 * hi0 + w2 * hi1], axis=-1)
    return x + g2_ref[...] * m


def _params(*sem):
    return pltpu.CompilerParams(dimension_semantics=sem, vmem_limit_bytes=VMEM_LIMIT)


def _mod_body(c_ref, w_ref, b_ref, ml_ref, o_ref):
    c = c_ref[...]
    sc = c * jax.nn.sigmoid(c)
    r = jnp.dot(sc, w_ref[...], preferred_element_type=F32,
                precision=lax.Precision.HIGHEST) + b_ref[...]
    for l in range(o_ref.shape[0]):
        o_ref[l] = r + ml_ref[l:l + 1, :]


def _modulation(c, w_mod, b_mod, mod_layer):
    b, d = c.shape
    depth, w = mod_layer.shape
    tn = 1024
    return pl.pallas_call(
        _mod_body,
        out_shape=jax.ShapeDtypeStruct((depth, b, w), F32),
        grid=(w // tn,),
        in_specs=[pl.BlockSpec((b, d), lambda j: (0, 0)),
                  pl.BlockSpec((d, tn), lambda j: (0, j)),
                  pl.BlockSpec((1, tn), lambda j: (0, j)),
                  pl.BlockSpec((depth, tn), lambda j: (0, j))],
        out_specs=pl.BlockSpec((depth, b, tn), lambda j: (0, 0, j)),
        compiler_params=_params("arbitrary"),
        name="modulation",
    )(c, w_mod, b_mod.reshape(1, w), mod_layer)


def _mix_in_body(has_moe, *refs):
    refs = list(refs)
    x_ref = refs.pop(0)
    if has_moe:
        m0_ref, m1_ref, rw_ref, g2_ref = refs.pop(0), refs.pop(0), refs.pop(0), refs.pop(0)
    (mod_ref, g1_ref, win_ref, gmg_ref, ws_ref, bs_ref, qg_ref, kg_ref, og_ref) = refs[:9]
    refs = refs[9:]
    if has_moe:
        xo_ref = refs.pop(0)
    ya_ref, q_ref, k_ref, v_ref = refs

    x = x_ref[...]
    if has_moe:
        x = _moe_residual(x, m0_ref, m1_ref, rw_ref, g2_ref)
        xo_ref[...] = x
    tm = x.shape[0]
    dq = ya_ref.shape[1]
    shift1, scale1 = mod_ref[0], mod_ref[1]
    h = _rms(x) * g1_ref[...] * (1.0 + scale1) + shift1
    proj = jnp.dot(h.astype(BF16), win_ref[...], preferred_element_type=F32)

    row = lax.broadcasted_iota(jnp.int32, (LANES, LANES), 0)
    col = lax.broadcasted_iota(jnp.int32, (LANES, LANES), 1)
    causal = row >= col
    for g in range(dq // LANES):
        sl = slice(g * LANES, (g + 1) * LANES)
        u = _gelu(proj[:, g * LANES:(g + 1) * LANES])
        vg = _gelu(proj[:, dq + g * LANES:dq + (g + 1) * LANES])
        vg = (_rms(vg) * gmg_ref[:, sl]).astype(BF16)
        wg = jnp.where(causal, ws_ref[g], 0.0).astype(BF16)
        for c in range(tm // LANES):
            rs = slice(c * LANES, (c + 1) * LANES)
            s = jnp.dot(wg, vg[rs], preferred_element_type=F32) + bs_ref[g]
            ya = u[rs] * s
            ya_ref[rs, sl] = (_rms(ya) * og_ref[:, sl]).astype(BF16)
    for hd in range(dq // LANES):
        sl = slice(hd * LANES, (hd + 1) * LANES)
        qh = proj[:, 2 * dq + hd * LANES:2 * dq + (hd + 1) * LANES]
        kh = proj[:, 3 * dq + hd * LANES:3 * dq + (hd + 1) * LANES]
        q_ref[:, sl] = (_rms(qh) * qg_ref[...]).astype(BF16)
        k_ref[:, sl] = (_rms(kh) * kg_ref[...]).astype(BF16)
    v_ref[...] = proj[:, 4 * dq:].astype(BF16)


def _mix_in(x, moe, mod, g1, w_in, gm_g, gm_ws, gm_bs, q_g, k_g, out_g, *, seq, tm):
    n, d = x.shape
    dq = w_in.shape[1] // 5
    ng = dq // LANES
    per_b = seq // tm
    row_spec = pl.BlockSpec((tm, d), lambda i: (i, 0))
    const2 = lambda i: (0, 0)
    in_specs = [row_spec]
    args = [x]
    if moe is not None:
        m, rw, gate2 = moe
        nch = d // 2 // LANES
        in_specs += [pl.BlockSpec((tm * nch, LANES), lambda i: (i, 0)),
                     pl.BlockSpec((tm * nch, LANES), lambda i: (n // tm + i, 0)),
                     pl.BlockSpec((tm, rw.shape[1]), lambda i: (i, 0)),
                     pl.BlockSpec((None, 1, d), lambda i: (i // per_b, 0, 0))]
        args += [m, m, rw, gate2]
    in_specs += [pl.BlockSpec((None, N_MOD, 1, d), lambda i: (i // per_b, 0, 0, 0)),
                 pl.BlockSpec((1, d), const2),
                 pl.BlockSpec(w_in.shape, const2),
                 pl.BlockSpec((1, dq), const2),
                 pl.BlockSpec((ng, LANES, LANES), lambda i: (0, 0, 0)),
                 pl.BlockSpec((ng, LANES, 1), lambda i: (0, 0, 0)),
                 pl.BlockSpec((1, LANES), const2),
                 pl.BlockSpec((1, LANES), const2),
                 pl.BlockSpec((1, dq), const2)]
    args += [mod, g1.reshape(1, d), w_in, gm_g.reshape(1, dq), gm_ws, gm_bs.reshape(ng, LANES, 1),
             q_g.reshape(1, LANES), k_g.reshape(1, LANES), out_g.reshape(1, -1)]
    act = jax.ShapeDtypeStruct((n, dq), BF16)
    act_spec = pl.BlockSpec((tm, dq), lambda i: (i, 0))
    out_shape = [act, act, act, act]
    out_specs = [act_spec, act_spec, act_spec, act_spec]
    if moe is not None:
        out_shape = [jax.ShapeDtypeStruct((n, d), F32)] + out_shape
        out_specs = [row_spec] + out_specs
    outs = pl.pallas_call(
        functools.partial(_mix_in_body, moe is not None),
        out_shape=out_shape, grid=(n // tm,), in_specs=in_specs, out_specs=out_specs,
        compiler_params=_params("arbitrary"), name="mix_in",
    )(*args)
    if moe is None:
        return (x,) + tuple(outs)
    return tuple(outs)


def _stick_break_body(q_ref, k_ref, v_ref, og_ref, o_ref, carry_ref, acc_ref):
    tq, dq = q_ref.shape
    nh = dq // LANES
    nsub = tq // LANES
    qi = pl.program_id(1)
    scale = LANES ** -0.5
    r2 = lax.broadcasted_iota(jnp.int32, (LANES, 2 * LANES), 0)
    c2 = lax.broadcasted_iota(jnp.int32, (LANES, 2 * LANES), 1)
    tri_ones = jnp.where((r2 > c2) | (c2 >= LANES), 1.0, 0.0).astype(BF16)

    carry_ref[...] = jnp.zeros_like(carry_ref)
    acc_ref[...] = jnp.zeros_like(acc_ref)

    def key_block(r0, kb, masked):
        rows = tq - r0
        heads = [slice(hd * LANES, (hd + 1) * LANES) for hd in range(nh)]
        start = pl.multiple_of(kb * LANES, LANES)
        if masked:
            rr = lax.broadcasted_iota(jnp.int32, (rows, LANES), 0)
            cc = lax.broadcasted_iota(jnp.int32, (rows, LANES), 1)
            keep = cc < rr
        zs = [lax.dot_general(q_ref[r0:, hs], k_ref[pl.ds(start, LANES), hs],
                              (((1,), (1,)), ((), ())), preferred_element_type=F32) * scale
              for hs in heads]
        lfs, wbs = [], []
        for z in zs:
            lf = -(jnp.maximum(z, 0.0) + jnp.log(1.0 + jnp.exp(-jnp.abs(z))))
            if masked:
                lf = jnp.where(keep, lf, 0.0)
            lfs.append(lf)
            wbs.append(jnp.dot(lf.astype(BF16), tri_ones, preferred_element_type=F32))
        for hd, hs in enumerate(heads):
            carry = carry_ref[hd, r0:, :]
            a = jnp.exp(lfs[hd] + zs[hd] + wbs[hd][:, :LANES] + carry)
            if masked:
                a = jnp.where(keep, a, 0.0)
            acc_ref[hd, r0:, :] += jnp.dot(a.astype(BF16), v_ref[pl.ds(start, LANES), hs],
                                           preferred_element_type=F32)
            carry_ref[hd, r0:, :] = carry + wbs[hd][:, LANES:]

    for j in reversed(range(nsub)):
        key_block(j * LANES, qi * nsub + j, True)

    def more(state):
        kb, live = state
        return jnp.logical_and(kb >= 0, live)

    def step(state):
        kb, _ = state
        key_block(0, kb, False)
        return kb - 1, jnp.max(carry_ref[...]) > LOG_WEIGHT_FLOOR

    lax.while_loop(more, step, (qi * nsub - 1, jnp.max(carry_ref[...]) > LOG_WEIGHT_FLOOR))
    for hd in range(nh):
        hs = slice(hd * LANES, (hd + 1) * LANES)
        o_ref[:, hs] = (_rms(acc_ref[hd]) * og_ref[:, hs]).astype(BF16)


def _stick_break(q, k, v, out_g, *, batch, seq, tq):
    n, dq = q.shape
    nh = dq // LANES
    per_b = seq // tq
    q_spec = pl.BlockSpec((tq, dq), lambda b, i: (b * per_b + i, 0))
    kv_spec = pl.BlockSpec((seq, dq), lambda b, i: (b, 0), pipeline_mode=pl.Buffered(1))
    return pl.pallas_call(
        _stick_break_body,
        out_shape=jax.ShapeDtypeStruct((n, dq), BF16),
        grid=(batch, per_b),
        in_specs=[q_spec, kv_spec, kv_spec, pl.BlockSpec((1, dq), lambda b, i: (0, 1))],
        out_specs=q_spec,
        scratch_shapes=[pltpu.VMEM((nh, tq, LANES), F32), pltpu.VMEM((nh, tq, LANES), F32)],
        compiler_params=_params("arbitrary", "arbitrary"),
        name="stick_break",
    )(q, k, v, out_g.reshape(1, -1))


def _mix_out_body(x_ref, ya_ref, yb_ref, mod_ref, woa_ref, wob_ref, g2_ref, wr_ref, br_ref,
                  x1_ref, h2_ref, ri_ref, rw_ref, cnt_ref, count_scr):
    tm = x_ref.shape[0]

    @pl.when(pl.program_id(0) == 0)
    def _():
        count_scr[...] = jnp.zeros_like(count_scr)

    gate1, shift2, scale2 = mod_ref[2], mod_ref[3], mod_ref[4]
    y = (jnp.dot(ya_ref[...], woa_ref[...], preferred_element_type=F32)
         + jnp.dot(yb_ref[...], wob_ref[...], preferred_element_type=F32))
    x1 = x_ref[...] + gate1 * y
    x1_ref[...] = x1
    h2 = _rms(x1) * g2_ref[...] * (1.0 + scale2) + shift2
    _store_token_major(h2_ref, 0, _pack_halves(h2))

    h_hi = h2.astype(BF16)
    h_lo = (h2 - h_hi.astype(F32)).astype(BF16)
    p = jnp.dot(h_hi, wr_ref[...], preferred_element_type=F32)
    lg = (p[:, :LANES] + p[:, LANES:]
          + jnp.dot(h_lo, wr_ref[:, :LANES], preferred_element_type=F32) + br_ref[...])
    lane = lax.broadcasted_iota(jnp.int32, (tm, LANES), 1)
    lane_f = lane.astype(F32)
    neg = -jnp.inf

    def first_max(vals):
        m = jnp.max(vals, axis=-1, keepdims=True)
        idx = jnp.min(jnp.where(vals == m, lane_f, float(LANES)), axis=-1, keepdims=True)
        return m, idx.astype(jnp.int32)

    is_group = lane < N_GROUPS
    gmax, gidx = first_max(jnp.where(is_group, lg, neg))
    p_group = 1.0 / jnp.sum(jnp.where(is_group, jnp.exp(lg - gmax), 0.0), axis=-1, keepdims=True)
    lo = N_GROUPS + gidx * EXPERTS_PER_GROUP
    el = jnp.where((lane >= lo) & (lane < lo + EXPERTS_PER_GROUP), lg, neg)
    m1, i1 = first_max(el)
    m2, i2 = first_max(jnp.where(lane == i1, neg, el))
    t = jnp.exp(m2 - m1)
    w1 = p_group / (1.0 + t)
    w2 = w1 * t
    e1 = i1 - N_GROUPS
    e2 = i2 - N_GROUPS

    hit1 = lane == e1
    hit2 = lane == e2
    onehot = jnp.where(hit1 | hit2, 1.0, 0.0)
    rr = lax.broadcasted_iota(jnp.int32, (tm, tm), 0)
    cc = lax.broadcasted_iota(jnp.int32, (tm, tm), 1)
    earlier = jnp.where(rr > cc, 1.0, 0.0).astype(BF16)
    base = jnp.dot(earlier, onehot.astype(BF16), preferred_element_type=F32) + count_scr[...]
    r1 = jnp.sum(jnp.where(hit1, base, 0.0), axis=-1, keepdims=True).astype(jnp.int32)
    r2 = jnp.sum(jnp.where(hit2, base, 0.0), axis=-1, keepdims=True).astype(jnp.int32)
    count_scr[...] += jnp.sum(onehot, axis=0, keepdims=True)
    cnt_ref[...] = count_scr[...].astype(jnp.int32)

    l8 = lax.broadcasted_iota(jnp.int32, ri_ref.shape, 1)
    ri_ref[...] = jnp.where(l8 == 0, e1, jnp.where(l8 == 1, e2, jnp.where(l8 == 2, r1, r2)))
    rw_ref[...] = jnp.where(l8 == 0, w1, w2)


def _mix_out(x, ya, yb, mod, w_out, g2, w_router, b_router, *, seq, tm):
    n, d = x.shape
    dq = ya.shape[1]
    nch = d // 2 // LANES
    per_b = seq // tm
    row_spec = pl.BlockSpec((tm, d), lambda i: (i, 0))
    act_spec = pl.BlockSpec((tm, dq), lambda i: (i, 0))
    small_spec = pl.BlockSpec((tm, 8), lambda i: (i, 0))
    const2 = lambda i: (0, 0)
    return pl.pallas_call(
        _mix_out_body,
        out_shape=[jax.ShapeDtypeStruct((n, d), F32),
                   jax.ShapeDtypeStruct((n * nch, LANES), jnp.uint32),
                   jax.ShapeDtypeStruct((n, 8), jnp.int32), jax.ShapeDtypeStruct((n, 8), F32),
                   jax.ShapeDtypeStruct((1, LANES), jnp.int32)],
        grid=(n // tm,),
        in_specs=[row_spec, act_spec, act_spec,
                  pl.BlockSpec((None, N_MOD, 1, d), lambda i: (i // per_b, 0, 0, 0)),
                  pl.BlockSpec((dq, d), const2),
                  pl.BlockSpec((dq, d), lambda i: (1, 0)),
                  pl.BlockSpec((1, d), const2),
                  pl.BlockSpec((d, 2 * LANES), const2),
                  pl.BlockSpec((1, LANES), const2)],
        out_specs=[row_spec, pl.BlockSpec((tm * nch, LANES), lambda i: (i, 0)), small_spec, small_spec,
                   pl.BlockSpec((1, LANES), const2)],
        scratch_shapes=[pltpu.VMEM((1, LANES), F32)],
        compiler_params=_params("arbitrary"),
        name="mix_out",
    )(x, ya, yb, mod, w_out, w_out, g2.reshape(1, d), w_router, b_router)


def _send_rows_body(slot_ref, ps_ref, pn_ref, h_ref, xs_hbm, zbuf, sem, zsem):
    nch = zbuf.shape[0]
    tm = h_ref.shape[0] // nch
    i = pl.program_id(0)
    n_tok = slot_ref.shape[0] // TOP_K
    n_pad = xs_hbm.shape[0] // nch - slot_ref.shape[0]

    def rows(ref, index):
        return ref.at[pl.ds(pl.multiple_of(index * nch, nch), nch), :]

    @pl.when(i == 0)
    def _():
        zbuf[...] = jnp.zeros_like(zbuf)

        def region(e, carry):
            first = ps_ref[e]

            def fill(j, c):
                pltpu.make_async_copy(zbuf, rows(xs_hbm, first + j), zsem).start()
                return c

            return lax.fori_loop(0, pn_ref[e], fill, carry)

        lax.fori_loop(0, ps_ref.shape[0], region, 0)
        pltpu.make_async_copy(xs_hbm.at[pl.ds(0, n_pad * nch), :],
                              xs_hbm.at[pl.ds(0, n_pad * nch), :], zsem).wait()

    def group(j, carry):
        for u in range(ROW_GROUP):
            r = j * ROW_GROUP + u
            for k in range(TOP_K):
                dst = slot_ref[k * n_tok + i * tm + r]
                pltpu.make_async_copy(rows(h_ref, r), rows(xs_hbm, dst), sem).start(priority=k)
        return carry

    lax.fori_loop(0, tm // ROW_GROUP, group, 0)
    pltpu.make_async_copy(xs_hbm.at[pl.ds(0, TOP_K * tm * nch), :],
                          xs_hbm.at[pl.ds(0, TOP_K * tm * nch), :], sem).wait()


def _send_rows(h2, slot, pad_start, pad_len, *, cap, tm):
    nch = h2.shape[0] * TOP_K // slot.shape[0]
    n = h2.shape[0] // nch
    assert tm % ROW_GROUP == 0 and cap - slot.shape[0] > 0
    return pl.pallas_call(
        _send_rows_body,
        out_shape=jax.ShapeDtypeStruct((cap * nch, LANES), jnp.uint32),
        grid_spec=pltpu.PrefetchScalarGridSpec(
            num_scalar_prefetch=3, grid=(n // tm,),
            in_specs=[pl.BlockSpec((tm * nch, LANES), lambda i, *_: (i, 0))],
            out_specs=pl.BlockSpec(memory_space=pl.ANY),
            scratch_shapes=[pltpu.VMEM((nch, LANES), jnp.uint32),
                            pltpu.SemaphoreType.DMA(()), pltpu.SemaphoreType.DMA(())]),
        compiler_params=_params("arbitrary"),
        name="moe_send_rows",
    )(slot, pad_start, pad_len, h2)


def _experts_body(slot_ref, be_ref, nv_ref, nu_ref, x_ref, zeros_hbm, wg_ref, wu_ref, wd_ref,
                  out_hbm, code_ref, ybuf, wg_b, wu_b, wd_b, ssem, zsem):
    nch = wg_b.shape[0] // 2 // LANES
    te = ybuf.shape[0] // nch
    i = pl.program_id(0)
    n_used = nu_ref[0]
    n_assign = slot_ref.shape[0]

    def token_rows(ref, index):
        start = index * nch
        if not isinstance(index, int):
            start = pl.multiple_of(start, nch)
        return ref.at[pl.ds(start, nch), :]

    def wait_scatter(rows):
        for b in reversed(range(te.bit_length())):
            size = (1 << b) * nch

            @pl.when((rows & (1 << b)) != 0)
            def _():
                pltpu.make_async_copy(ybuf.at[pl.ds(0, size), :], out_hbm.at[pl.ds(0, size), :],
                                      ssem).wait()

    @pl.when(i == 0)
    def _():
        fill = pltpu.make_async_copy(zeros_hbm, code_ref, zsem)
        fill.start()
        fill.wait()

        def invert(j, carry):
            for u in range(ROW_GROUP):
                a = j * ROW_GROUP + u
                code_ref[slot_ref[a]] = a
            return carry

        lax.fori_loop(0, n_assign // ROW_GROUP, invert, 0)

    @pl.when(i < n_used)
    def _():
        base = i * te
        n_valid = nv_ref[i]

        @pl.when(jnp.logical_or(i == 0, be_ref[i] != be_ref[jnp.maximum(i - 1, 0)]))
        def _():
            wg_b[...] = wg_ref[...].astype(BF16)
            wu_b[...] = wu_ref[...].astype(BF16)
            wd_b[...] = wd_ref[...].astype(BF16)

        half = nch * LANES
        x_lo, x_hi = _unpack_halves(_load_token_major(x_ref, 0, te, nch))
        x_lo, x_hi = x_lo.astype(BF16), x_hi.astype(BF16)
        g = (jnp.dot(x_lo, wg_b[:half], preferred_element_type=F32)
             + jnp.dot(x_hi, wg_b[half:], preferred_element_type=F32))
        u = (jnp.dot(x_lo, wu_b[:half], preferred_element_type=F32)
             + jnp.dot(x_hi, wu_b[half:], preferred_element_type=F32))
        act = (g * jax.nn.sigmoid(g) * u).astype(BF16)

        @pl.when(i > 0)
        def _():
            wait_scatter(nv_ref[jnp.maximum(i - 1, 0)])

        _store_token_major(ybuf, 0, _pack_halves(jnp.dot(act, wd_b[...],
                                                          preferred_element_type=F32)))

        def scatter(r, priority):
            pltpu.make_async_copy(token_rows(ybuf, r), token_rows(out_hbm, code_ref[base + r]),
                                  ssem).start(priority=priority)

        n_groups = lax.shift_right_logical(n_valid, ROW_GROUP.bit_length() - 1)
        for j in range(te // ROW_GROUP):
            @pl.when(j < n_groups)
            def _():
                for u in range(ROW_GROUP):
                    scatter(j * ROW_GROUP + u, u % 2)

        def scatter_one(r, carry):
            scatter(r, 0)
            return carry

        lax.fori_loop(n_groups * ROW_GROUP, n_valid, scatter_one, 0)

        @pl.when(i == n_used - 1)
        def _():
            wait_scatter(n_valid)


def _experts(xs, slot, block_e, n_valid, n_used, w_gate, w_up, w_down, *, layer, te):
    d, de = w_gate.shape[2:]
    nch = d // 2 // LANES
    assert nch % 8 == 0, "a token must cover whole (8, 128) tiles"
    n_blocks = block_e.shape[0]
    cap = n_blocks * te
    assert slot.shape[0] % ROW_GROUP == 0 and te % ROW_GROUP == 0
    by_expert = lambda i, slot, be, nv, nu: (layer, be[i], 0, 0)
    grid_spec = pltpu.PrefetchScalarGridSpec(
        num_scalar_prefetch=4,
        grid=(n_blocks,),
        in_specs=[pl.BlockSpec((te * nch, LANES),
                               lambda i, slot, be, nv, nu: (jnp.minimum(i, nu[0] - 1), 0)),
                  pl.BlockSpec(memory_space=pl.ANY),
                  pl.BlockSpec((None, None, d, de), by_expert),
                  pl.BlockSpec((None, None, d, de), by_expert),
                  pl.BlockSpec((None, None, de, d), by_expert)],
        out_specs=pl.BlockSpec(memory_space=pl.ANY),
        scratch_shapes=[pltpu.SMEM((cap,), jnp.int32),
                        pltpu.VMEM((te * nch, LANES), jnp.uint32),
                        pltpu.VMEM((d, de), BF16), pltpu.VMEM((d, de), BF16),
                        pltpu.VMEM((de, d), BF16),
                        pltpu.SemaphoreType.DMA(()), pltpu.SemaphoreType.DMA(())],
    )
    return pl.pallas_call(
        _experts_body,
        out_shape=jax.ShapeDtypeStruct((slot.shape[0] * nch, LANES), jnp.uint32),
        grid_spec=grid_spec,
        compiler_params=_params("arbitrary"),
        name="moe_experts",
    )(slot, block_e, n_valid, n_used, xs, jnp.zeros((cap,), jnp.int32), w_gate, w_up, w_down)


def _dispatch(ri, counts, *, te):
    n = ri.shape[0]
    counts = counts[0, :N_EXPERTS]
    padded = (counts + te - 1) // te * te
    pend = jnp.cumsum(padded)
    pstart = pend - padded
    n_blocks = TOP_K * n // te + N_EXPERTS
    cap = n_blocks * te
    expert = ri[:, :TOP_K].T[:, :, None]
    first = jnp.sum(jnp.where(expert == jnp.arange(N_EXPERTS), pstart, 0), axis=-1)
    slot = (first + ri[:, TOP_K:2 * TOP_K].T).reshape(-1).astype(jnp.int32)
    block_start = jnp.arange(n_blocks, dtype=jnp.int32) * te
    block_e = jnp.minimum(jnp.sum(pend[None, :] <= block_start[:, None], axis=1),
                          N_EXPERTS - 1).astype(jnp.int32)
    n_valid = jnp.clip(pstart[block_e] + counts[block_e] - block_start, 0, te).astype(jnp.int32)
    n_used = (pend[-1:] // te).astype(jnp.int32)
    pad_start = jnp.concatenate([pstart + counts, pend[-1:]]).astype(jnp.int32)
    pad_len = jnp.concatenate([padded - counts, cap - pend[-1:]]).astype(jnp.int32)
    return slot, block_e, n_valid, n_used, pad_start, pad_len


def _combine_body(x_ref, m0_ref, m1_ref, rw_ref, g2_ref, o_ref):
    o_ref[...] = _moe_residual(x_ref[...], m0_ref, m1_ref, rw_ref, g2_ref)


def _combine(x, m, rw, gate2, *, seq, tm):
    n, d = x.shape
    nch = d // 2 // LANES
    per_b = seq // tm
    row_spec = pl.BlockSpec((tm, d), lambda i: (i, 0))
    return pl.pallas_call(
        _combine_body,
        out_shape=jax.ShapeDtypeStruct((n, d), F32),
        grid=(n // tm,),
        in_specs=[row_spec,
                  pl.BlockSpec((tm * nch, LANES), lambda i: (i, 0)),
                  pl.BlockSpec((tm * nch, LANES), lambda i: (n // tm + i, 0)),
                  pl.BlockSpec((tm, rw.shape[1]), lambda i: (i, 0)),
                  pl.BlockSpec((None, 1, d), lambda i: (i // per_b, 0, 0))],
        out_specs=row_spec,
        compiler_params=_params("arbitrary"),
        name="moe_combine",
    )(x, m, m, rw, gate2)


def _tiles(seq):
    tm = min(512, seq)
    tq = min(256, seq)
    te = 256
    return tm, tq, te


def kernel(x, c, w_mod, b_mod, mod_layer, norm1_g, w_in, gm_norm_g, gm_ws, gm_bs, q_norm_g, k_norm_g, out_norm_g, w_out, norm2_g, w_group, b_group, w_route, b_route, w_gate, w_up, w_down):
    batch, seq, d = x.shape
    depth = mod_layer.shape[0]
    n = batch * seq
    tm, tq, te = _tiles(seq)
    assert seq % tm == 0 and seq % tq == 0 and tm % LANES == 0 and tq % LANES == 0
    assert (TOP_K * n) % te == 0 and w_group.shape[2] == N_GROUPS and w_route.shape[2] == N_EXPERTS
    assert w_in.shape[2] * 2 == 5 * w_out.shape[1]

    mod_all = _modulation(c, w_mod, b_mod, mod_layer).reshape(depth, batch, N_MOD, 1, d)
    w_in_b, w_out_b = w_in.astype(BF16), w_out.astype(BF16)
    pad = LANES - N_GROUPS - N_EXPERTS
    w_router = jnp.pad(jnp.concatenate([w_group, w_route], axis=2), ((0, 0), (0, 0), (0, pad)))
    w_router_hi = w_router.astype(BF16)
    w_router_lo = (w_router - w_router_hi.astype(F32)).astype(BF16)
    w_router = jnp.concatenate([w_router_hi, w_router_lo], axis=2)
    b_router = jnp.pad(jnp.concatenate([b_group, b_route], axis=1), ((0, 0), (0, pad)))

    xf = x.reshape(n, d)
    moe = None
    for l in range(depth):
        xf, ya, q, k, v = _mix_in(xf, moe, mod_all[l], norm1_g[l], w_in_b[l], gm_norm_g[l], gm_ws[l],
                                  gm_bs[l], q_norm_g[l], k_norm_g[l], out_norm_g[l], seq=seq, tm=tm)
        yb = _stick_break(q, k, v, out_norm_g[l], batch=batch, seq=seq, tq=tq)
        xf, h2, ri, rw, counts = _mix_out(xf, ya, yb, mod_all[l], w_out_b[l], norm2_g[l],
                                          w_router[l], b_router[l:l + 1], seq=seq, tm=tm)
        slot, block_e, n_valid, n_used, pad_start, pad_len = _dispatch(ri, counts, te=te)
        xs = _send_rows(h2, slot, pad_start, pad_len, cap=block_e.shape[0] * te, tm=tm)
        m = _experts(xs, slot, block_e, n_valid, n_used, w_gate, w_up, w_down, layer=l, te=te)
        moe = (m, rw, mod_all[l][:, 5])
    out = _combine(xf, *moe, seq=seq, tm=tm)
    return out.reshape(batch, seq, d)
```

```python
import functools

import jax
import jax.numpy as jnp
from jax import lax
from jax.experimental import pallas as pl
from jax.experimental.pallas import tpu as pltpu

F32 = jnp.float32
BF16 = jnp.bfloat16

EPS = 1e-6
LANES = 128
N_MOD = 6
N_GROUPS = 4
EXPERTS_PER_GROUP = 8
N_EXPERTS = N_GROUPS * EXPERTS_PER_GROUP
TOP_K = 2
LOG_WEIGHT_FLOOR = -104.0
ROW_GROUP = 8
VMEM_LIMIT = 56 * 1024 * 1024
HIGH_HALF = 0xFFFF0000


def _rms(x):
    return x * lax.rsqrt(jnp.mean(x * x, axis=-1, keepdims=True) + EPS)


def _gelu(x):
    return 0.5 * x * (1.0 + lax.erf(x * (2.0 ** -0.5)))


def _pack_halves(x):
    half = x.shape[1] // 2
    bits = lax.bitcast_convert_type(x.astype(BF16).astype(F32), jnp.uint32)
    return (lax.shift_right_logical(bits[:, :half], jnp.uint32(16))
            | (bits[:, half:] & jnp.uint32(HIGH_HALF)))


def _unpack_halves(w):
    lo = lax.bitcast_convert_type(lax.shift_left(w, jnp.uint32(16)), F32)
    hi = lax.bitcast_convert_type(w & jnp.uint32(HIGH_HALF), F32)
    return lo, hi


def _store_token_major(ref, base, packed):
    rows, width = packed.shape
    nch = width // LANES
    for c in range(nch):
        ref[pl.ds(base + c, rows, stride=nch), :] = packed[:, c * LANES:(c + 1) * LANES]


def _load_token_major(ref, base, rows, nch):
    return jnp.concatenate([ref[pl.ds(base + c, rows, stride=nch), :] for c in range(nch)], axis=-1)


def _moe_residual(x, m0_ref, m1_ref, rw_ref, g2_ref, r0=0):
    rows, d = x.shape
    nch = d // 2 // LANES
    lo0, hi0 = _unpack_halves(_load_token_major(m0_ref, r0 * nch, rows, nch))
    lo1, hi1 = _unpack_halves(_load_token_major(m1_ref, r0 * nch, rows, nch))
    w1, w2 = rw_ref[r0:r0 + rows, 0:1], rw_ref[r0:r0 + rows, 1:2]
    m = jnp.concatenate([w1 * lo0 + w2 * lo1, w1 * hi0 + w2 * hi1], axis=-1)
    return x + g2_ref[...] * m


def _params(*sem):
    return pltpu.CompilerParams(dimension_semantics=sem, vmem_limit_bytes=VMEM_LIMIT)


def _mod_body(c_ref, w_ref, b_ref, ml_ref, o_ref):
    c = c_ref[...]
    sc = c * jax.nn.sigmoid(c)
    r = jnp.dot(sc, w_ref[...], preferred_element_type=F32,
                precision=lax.Precision.HIGHEST) + b_ref[...]
    for l in range(o_ref.shape[0]):
        o_ref[l] = r + ml_ref[l:l + 1, :]


def _modulation(c, w_mod, b_mod, mod_layer):
    b, d = c.shape
    depth, w = mod_layer.shape
    tn = 1024
    return pl.pallas_call(
        _mod_body,
        out_shape=jax.ShapeDtypeStruct((depth, b, w), F32),
        grid=(w // tn,),
        in_specs=[pl.BlockSpec((b, d), lambda j: (0, 0)),
                  pl.BlockSpec((d, tn), lambda j: (0, j)),
                  pl.BlockSpec((1, tn), lambda j: (0, j)),
                  pl.BlockSpec((depth, tn), lambda j: (0, j))],
        out_specs=pl.BlockSpec((depth, b, tn), lambda j: (0, 0, j)),
        compiler_params=_params("arbitrary"),
        name="modulation",
    )(c, w_mod, b_mod.reshape(1, w), mod_layer)


def _mix_in_body(has_moe, *refs):
    refs = list(refs)
    x_ref = refs.pop(0)
    if has_moe:
        m0_ref, m1_ref, rw_ref, g2_ref = refs.pop(0), refs.pop(0), refs.pop(0), refs.pop(0)
    (mod_ref, g1_ref, win_ref, gmg_ref, ws_ref, bs_ref, qg_ref, kg_ref, og_ref) = refs[:9]
    refs = refs[9:]
    if has_moe:
        xo_ref = refs.pop(0)
    ya_ref, q_ref, k_ref, v_ref = refs

    tm = x_ref.shape[0]
    dq = ya_ref.shape[1]
    shift1, scale1 = mod_ref[0], mod_ref[1]
    gain1 = g1_ref[...] * (1.0 + scale1)
    row = lax.broadcasted_iota(jnp.int32, (LANES, LANES), 0)
    col = lax.broadcasted_iota(jnp.int32, (LANES, LANES), 1)
    causal = row >= col

    def normed(r0, rows):
        x = x_ref[r0:r0 + rows, :]
        if has_moe:
            x = _moe_residual(x, m0_ref, m1_ref, rw_ref, g2_ref, r0)
            xo_ref[r0:r0 + rows, :] = x
        return (_rms(x) * gain1 + shift1).astype(BF16)

    def finish(r0, rows, proj):
        for g in range(dq // LANES):
            sl = slice(g * LANES, (g + 1) * LANES)
            u = _gelu(proj[:, g * LANES:(g + 1) * LANES])
            vg = _gelu(proj[:, dq + g * LANES:dq + (g + 1) * LANES])
            vg = (_rms(vg) * gmg_ref[:, sl]).astype(BF16)
            wg = jnp.where(causal, ws_ref[g], 0.0).astype(BF16)
            for c in range(rows // LANES):
                rs = slice(c * LANES, (c + 1) * LANES)
                s = jnp.dot(wg, vg[rs], preferred_element_type=F32) + bs_ref[g]
                ya = u[rs] * s
                ya_ref[r0 + c * LANES:r0 + (c + 1) * LANES, sl] = (
                    _rms(ya) * og_ref[:, sl]).astype(BF16)
        for hd in range(dq // LANES):
            sl = slice(hd * LANES, (hd + 1) * LANES)
            qh = proj[:, 2 * dq + hd * LANES:2 * dq + (hd + 1) * LANES]
            kh = proj[:, 3 * dq + hd * LANES:3 * dq + (hd + 1) * LANES]
            q_ref[r0:r0 + rows, sl] = (_rms(qh) * qg_ref[...]).astype(BF16)
            k_ref[r0:r0 + rows, sl] = (_rms(kh) * kg_ref[...]).astype(BF16)
        v_ref[r0:r0 + rows, :] = proj[:, 4 * dq:].astype(BF16)

    n_parts = 2 if tm % (2 * LANES) == 0 else 1
    rows = tm // n_parts
    hs = [normed(p * rows, rows) for p in range(n_parts)]
    projs = [jnp.dot(h, win_ref[...], preferred_element_type=F32) for h in hs]
    for p in range(n_parts):
        finish(p * rows, rows, projs[p])


def _mix_in(x, moe, mod, g1, w_in, gm_g, gm_ws, gm_bs, q_g, k_g, out_g, *, seq, tm):
    n, d = x.shape
    dq = w_in.shape[1] // 5
    ng = dq // LANES
    per_b = seq // tm
    row_spec = pl.BlockSpec((tm, d), lambda i: (i, 0))
    const2 = lambda i: (0, 0)
    in_specs = [row_spec]
    args = [x]
    if moe is not None:
        m, rw, gate2 = moe
        nch = d // 2 // LANES
        in_specs += [pl.BlockSpec((tm * nch, LANES), lambda i: (i, 0)),
                     pl.BlockSpec((tm * nch, LANES), lambda i: (n // tm + i, 0)),
                     pl.BlockSpec((tm, rw.shape[1]), lambda i: (i, 0)),
                     pl.BlockSpec((None, 1, d), lambda i: (i // per_b, 0, 0))]
        args += [m, m, rw, gate2]
    in_specs += [pl.BlockSpec((None, N_MOD, 1, d), lambda i: (i // per_b, 0, 0, 0)),
                 pl.BlockSpec((1, d), const2),
                 pl.BlockSpec(w_in.shape, const2),
                 pl.BlockSpec((1, dq), const2),
                 pl.BlockSpec((ng, LANES, LANES), lambda i: (0, 0, 0)),
                 pl.BlockSpec((ng, LANES, 1), lambda i: (0, 0, 0)),
                 pl.BlockSpec((1, LANES), const2),
                 pl.BlockSpec((1, LANES), const2),
                 pl.BlockSpec((1, dq), const2)]
    args += [mod, g1.reshape(1, d), w_in, gm_g.reshape(1, dq), gm_ws, gm_bs.reshape(ng, LANES, 1),
             q_g.reshape(1, LANES), k_g.reshape(1, LANES), out_g.reshape(1, -1)]
    act = jax.ShapeDtypeStruct((n, dq), BF16)
    act_spec = pl.BlockSpec((tm, dq), lambda i: (i, 0))
    out_shape = [act, act, act, act]
    out_specs = [act_spec, act_spec, act_spec, act_spec]
    if moe is not None:
        out_shape = [jax.ShapeDtypeStruct((n, d), F32)] + out_shape
        out_specs = [row_spec] + out_specs
    outs = pl.pallas_call(
        functools.partial(_mix_in_body, moe is not None),
        out_shape=out_shape, grid=(n // tm,), in_specs=in_specs, out_specs=out_specs,
        compiler_params=_params("arbitrary"), name="mix_in",
    )(*args)
    if moe is None:
        return (x,) + tuple(outs)
    return tuple(outs)


def _stick_break_body(q_ref, k_ref, v_ref, og_ref, o_ref, carry_ref, acc_ref):
    tq, dq = q_ref.shape
    nh = dq // LANES
    nsub = tq // LANES
    qi = pl.program_id(1)
    scale = LANES ** -0.5
    r2 = lax.broadcasted_iota(jnp.int32, (LANES, 2 * LANES), 0)
    c2 = lax.broadcasted_iota(jnp.int32, (LANES, 2 * LANES), 1)
    tri_ones = jnp.where((r2 > c2) | (c2 >= LANES), 1.0, 0.0).astype(BF16)

    carry_ref[...] = jnp.zeros_like(carry_ref)
    acc_ref[...] = jnp.zeros_like(acc_ref)

    def key_block(r0, kb, masked):
        rows = tq - r0
        heads = [slice(hd * LANES, (hd + 1) * LANES) for hd in range(nh)]
        start = pl.multiple_of(kb * LANES, LANES)
        if masked:
            rr = lax.broadcasted_iota(jnp.int32, (rows, LANES), 0)
            cc = lax.broadcasted_iota(jnp.int32, (rows, LANES), 1)
            keep = cc < rr
        zs = [lax.dot_general(q_ref[r0:, hs], k_ref[pl.ds(start, LANES), hs],
                              (((1,), (1,)), ((), ())), preferred_element_type=F32) * scale
              for hs in heads]
        lfs, wbs = [], []
        for z in zs:
            lf = -(jnp.maximum(z, 0.0) + jnp.log(1.0 + jnp.exp(-jnp.abs(z))))
            if masked:
                lf = jnp.where(keep, lf, 0.0)
            lfs.append(lf)
            wbs.append(jnp.dot(lf.astype(BF16), tri_ones, preferred_element_type=F32))
        for hd, hs in enumerate(heads):
            carry = carry_ref[hd, r0:, :]
            a = jnp.exp(lfs[hd] + zs[hd] + wbs[hd][:, :LANES] + carry)
            if masked:
                a = jnp.where(keep, a, 0.0)
            acc_ref[hd, r0:, :] += jnp.dot(a.astype(BF16), v_ref[pl.ds(start, LANES), hs],
                                           preferred_element_type=F32)
            carry_ref[hd, r0:, :] = carry + wbs[hd][:, LANES:]

    for j in reversed(range(nsub)):
        key_block(j * LANES, qi * nsub + j, True)

    def more(state):
        kb, live = state
        return jnp.logical_and(kb >= 0, live)

    def step(state):
        kb, _ = state
        key_block(0, kb, False)
        return kb - 1, jnp.max(carry_ref[...]) > LOG_WEIGHT_FLOOR

    lax.while_loop(more, step, (qi * nsub - 1, jnp.max(carry_ref[...]) > LOG_WEIGHT_FLOOR))
    for hd in range(nh):
        hs = slice(hd * LANES, (hd + 1) * LANES)
        o_ref[:, hs] = (_rms(acc_ref[hd]) * og_ref[:, hs]).astype(BF16)


def _stick_break(q, k, v, out_g, *, batch, seq, tq):
    n, dq = q.shape
    nh = dq // LANES
    per_b = seq // tq
    q_spec = pl.BlockSpec((tq, dq), lambda b, i: (b * per_b + i, 0))
    kv_spec = pl.BlockSpec((seq, dq), lambda b, i: (b, 0), pipeline_mode=pl.Buffered(1))
    return pl.pallas_call(
        _stick_break_body,
        out_shape=jax.ShapeDtypeStruct((n, dq), BF16),
        grid=(batch, per_b),
        in_specs=[q_spec, kv_spec, kv_spec, pl.BlockSpec((1, dq), lambda b, i: (0, 1))],
        out_specs=q_spec,
        scratch_shapes=[pltpu.VMEM((nh, tq, LANES), F32), pltpu.VMEM((nh, tq, LANES), F32)],
        compiler_params=_params("arbitrary", "arbitrary"),
        name="stick_break",
    )(q, k, v, out_g.reshape(1, -1))


def _mix_out_body(x_ref, ya_ref, yb_ref, mod_ref, woa_ref, wob_ref, g2_ref, wr_ref, br_ref,
                  x1_ref, h2_ref, ri_ref, rw_ref, cnt_ref, count_scr):
    tm = x_ref.shape[0]

    @pl.when(pl.program_id(0) == 0)
    def _():
        count_scr[...] = jnp.zeros_like(count_scr)

    gate1, shift2, scale2 = mod_ref[2], mod_ref[3], mod_ref[4]
    y = (jnp.dot(ya_ref[...], woa_ref[...], preferred_element_type=F32)
         + jnp.dot(yb_ref[...], wob_ref[...], preferred_element_type=F32))
    x1 = x_ref[...] + gate1 * y
    x1_ref[...] = x1
    h2 = _rms(x1) * g2_ref[...] * (1.0 + scale2) + shift2
    _store_token_major(h2_ref, 0, _pack_halves(h2))

    h_hi = h2.astype(BF16)
    h_lo = (h2 - h_hi.astype(F32)).astype(BF16)
    p = jnp.dot(h_hi, wr_ref[...], preferred_element_type=F32)
    lg = (p[:, :LANES] + p[:, LANES:]
          + jnp.dot(h_lo, wr_ref[:, :LANES], preferred_element_type=F32) + br_ref[...])
    lane = lax.broadcasted_iota(jnp.int32, (tm, LANES), 1)
    lane_f = lane.astype(F32)
    neg = -jnp.inf

    def first_max(vals):
        m = jnp.max(vals, axis=-1, keepdims=True)
        idx = jnp.min(jnp.where(vals == m, lane_f, float(LANES)), axis=-1, keepdims=True)
        return m, idx.astype(jnp.int32)

    is_group = lane < N_GROUPS
    gmax, gidx = first_max(jnp.where(is_group, lg, neg))
    p_group = 1.0 / jnp.sum(jnp.where(is_group, jnp.exp(lg - gmax), 0.0), axis=-1, keepdims=True)
    lo = N_GROUPS + gidx * EXPERTS_PER_GROUP
    el = jnp.where((lane >= lo) & (lane < lo + EXPERTS_PER_GROUP), lg, neg)
    m1, i1 = first_max(el)
    m2, i2 = first_max(jnp.where(lane == i1, neg, el))
    t = jnp.exp(m2 - m1)
    w1 = p_group / (1.0 + t)
    w2 = w1 * t
    e1 = i1 - N_GROUPS
    e2 = i2 - N_GROUPS

    hit1 = lane == e1
    hit2 = lane == e2
    onehot = jnp.where(hit1 | hit2, 1.0, 0.0)
    rr = lax.broadcasted_iota(jnp.int32, (tm, tm), 0)
    cc = lax.broadcasted_iota(jnp.int32, (tm, tm), 1)
    earlier = jnp.where(rr > cc, 1.0, 0.0).astype(BF16)
    base = jnp.dot(earlier, onehot.astype(BF16), preferred_element_type=F32) + count_scr[...]
    r1 = jnp.sum(jnp.where(hit1, base, 0.0), axis=-1, keepdims=True).astype(jnp.int32)
    r2 = jnp.sum(jnp.where(hit2, base, 0.0), axis=-1, keepdims=True).astype(jnp.int32)
    count_scr[...] += jnp.sum(onehot, axis=0, keepdims=True)
    cnt_ref[...] = count_scr[...].astype(jnp.int32)

    l8 = lax.broadcasted_iota(jnp.int32, ri_ref.shape, 1)
    ri_ref[...] = jnp.where(l8 == 0, e1, jnp.where(l8 == 1, e2, jnp.where(l8 == 2, r1, r2)))
    rw_ref[...] = jnp.where(l8 == 0, w1, w2)


def _mix_out(x, ya, yb, mod, w_out, g2, w_router, b_router, *, seq, tm):
    n, d = x.shape
    dq = ya.shape[1]
    nch = d // 2 // LANES
    per_b = seq // tm
    row_spec = pl.BlockSpec((tm, d), lambda i: (i, 0))
    act_spec = pl.BlockSpec((tm, dq), lambda i: (i, 0))
    small_spec = pl.BlockSpec((tm, 8), lambda i: (i, 0))
    const2 = lambda i: (0, 0)
    return pl.pallas_call(
        _mix_out_body,
        out_shape=[jax.ShapeDtypeStruct((n, d), F32),
                   jax.ShapeDtypeStruct((n * nch, LANES), jnp.uint32),
                   jax.ShapeDtypeStruct((n, 8), jnp.int32), jax.ShapeDtypeStruct((n, 8), F32),
                   jax.ShapeDtypeStruct((1, LANES), jnp.int32)],
        grid=(n // tm,),
        in_specs=[row_spec, act_spec, act_spec,
                  pl.BlockSpec((None, N_MOD, 1, d), lambda i: (i // per_b, 0, 0, 0)),
                  pl.BlockSpec((dq, d), const2),
                  pl.BlockSpec((dq, d), lambda i: (1, 0)),
                  pl.BlockSpec((1, d), const2),
                  pl.BlockSpec((d, 2 * LANES), const2),
                  pl.BlockSpec((1, LANES), const2)],
        out_specs=[row_spec, pl.BlockSpec((tm * nch, LANES), lambda i: (i, 0)), small_spec, small_spec,
                   pl.BlockSpec((1, LANES), const2)],
        scratch_shapes=[pltpu.VMEM((1, LANES), F32)],
        compiler_params=_params("arbitrary"),
        name="mix_out",
    )(x, ya, yb, mod, w_out, w_out, g2.reshape(1, d), w_router, b_router)


def _send_rows_body(slot_ref, ps_ref, pn_ref, h_ref, zeros_hbm, xs_hbm, code_hbm,
                    code_ref, zbuf, sem, zsem, csem):
    nch = zbuf.shape[0]
    tm = h_ref.shape[0] // nch
    i = pl.program_id(0)
    n_tok = slot_ref.shape[0] // TOP_K
    n_pad = xs_hbm.shape[0] // nch - slot_ref.shape[0]

    def rows(ref, index):
        return ref.at[pl.ds(pl.multiple_of(index * nch, nch), nch), :]

    @pl.when(i == 0)
    def _():
        fill_code = pltpu.make_async_copy(zeros_hbm, code_ref, csem)
        fill_code.start()
        fill_code.wait()
        zbuf[...] = jnp.zeros_like(zbuf)

        def region(e, carry):
            first = ps_ref[e]

            def fill(j, c):
                pltpu.make_async_copy(zbuf, rows(xs_hbm, first + j), zsem).start()
                return c

            return lax.fori_loop(0, pn_ref[e], fill, carry)

        lax.fori_loop(0, ps_ref.shape[0], region, 0)
        pltpu.make_async_copy(xs_hbm.at[pl.ds(0, n_pad * nch), :],
                              xs_hbm.at[pl.ds(0, n_pad * nch), :], zsem).wait()

    def group(j, carry):
        for u in range(ROW_GROUP):
            r = j * ROW_GROUP + u
            for k in range(TOP_K):
                a = k * n_tok + i * tm + r
                dst = slot_ref[a]
                code_ref[dst] = a
                pltpu.make_async_copy(rows(h_ref, r), rows(xs_hbm, dst), sem).start(priority=k)
        return carry

    lax.fori_loop(0, tm // ROW_GROUP, group, 0)
    pltpu.make_async_copy(xs_hbm.at[pl.ds(0, TOP_K * tm * nch), :],
                          xs_hbm.at[pl.ds(0, TOP_K * tm * nch), :], sem).wait()

    @pl.when(i == pl.num_programs(0) - 1)
    def _():
        emit_code = pltpu.make_async_copy(code_ref, code_hbm, csem)
        emit_code.start()
        emit_code.wait()


def _send_rows(h2, slot, pad_start, pad_len, *, cap, tm):
    nch = h2.shape[0] * TOP_K // slot.shape[0]
    n = h2.shape[0] // nch
    assert tm % ROW_GROUP == 0 and cap - slot.shape[0] > 0
    return pl.pallas_call(
        _send_rows_body,
        out_shape=[jax.ShapeDtypeStruct((cap * nch, LANES), jnp.uint32),
                   jax.ShapeDtypeStruct((cap,), jnp.int32)],
        grid_spec=pltpu.PrefetchScalarGridSpec(
            num_scalar_prefetch=3, grid=(n // tm,),
            in_specs=[pl.BlockSpec((tm * nch, LANES), lambda i, *_: (i, 0)),
                      pl.BlockSpec(memory_space=pl.ANY)],
            out_specs=[pl.BlockSpec(memory_space=pl.ANY), pl.BlockSpec(memory_space=pl.ANY)],
            scratch_shapes=[pltpu.SMEM((cap,), jnp.int32),
                            pltpu.VMEM((nch, LANES), jnp.uint32),
                            pltpu.SemaphoreType.DMA(()), pltpu.SemaphoreType.DMA(()),
                            pltpu.SemaphoreType.DMA(())]),
        compiler_params=_params("arbitrary"),
        name="moe_send_rows",
    )(slot, pad_start, pad_len, h2, jnp.zeros((cap,), jnp.int32))


def _experts_body(code_ref, be_ref, nv_ref, nu_ref, x_ref, wg_ref, wu_ref, wd_ref,
                  out_hbm, ybuf, wg_b, wu_b, wd_b, ssem):
    nch = wg_b.shape[0] // 2 // LANES
    te = ybuf.shape[0] // nch
    i = pl.program_id(0)
    n_used = nu_ref[0]

    def token_rows(ref, index):
        return ref.at[pl.ds(pl.multiple_of(index * nch, nch), nch), :]

    def wait_scatter(rows):
        for b in reversed(range(te.bit_length())):
            size = (1 << b) * nch

            @pl.when((rows & (1 << b)) != 0)
            def _():
                pltpu.make_async_copy(ybuf.at[pl.ds(0, size), :], out_hbm.at[pl.ds(0, size), :],
                                      ssem).wait()

    @pl.when(i < n_used)
    def _():
        base = i * te
        n_valid = nv_ref[i]

        @pl.when(jnp.logical_or(i == 0, be_ref[i] != be_ref[jnp.maximum(i - 1, 0)]))
        def _():
            wg_b[...] = wg_ref[...].astype(BF16)
            wu_b[...] = wu_ref[...].astype(BF16)
            wd_b[...] = wd_ref[...].astype(BF16)

        half = nch * LANES
        x_lo, x_hi = _unpack_halves(_load_token_major(x_ref, 0, te, nch))
        x_lo, x_hi = x_lo.astype(BF16), x_hi.astype(BF16)
        g = (jnp.dot(x_lo, wg_b[:half], preferred_element_type=F32)
             + jnp.dot(x_hi, wg_b[half:], preferred_element_type=F32))
        u = (jnp.dot(x_lo, wu_b[:half], preferred_element_type=F32)
             + jnp.dot(x_hi, wu_b[half:], preferred_element_type=F32))
        act = (g * jax.nn.sigmoid(g) * u).astype(BF16)

        @pl.when(i > 0)
        def _():
            wait_scatter(nv_ref[jnp.maximum(i - 1, 0)])

        _store_token_major(ybuf, 0, _pack_halves(jnp.dot(act, wd_b[...],
                                                          preferred_element_type=F32)))

        def scatter(r, priority):
            pltpu.make_async_copy(token_rows(ybuf, r), token_rows(out_hbm, code_ref[base + r]),
                                  ssem).start(priority=priority)

        def scatter_group(j, carry):
            for u in range(ROW_GROUP):
                scatter(j * ROW_GROUP + u, u % 2)
            return carry

        def scatter_one(r, carry):
            scatter(r, 0)
            return carry

        n_groups = lax.shift_right_logical(n_valid, ROW_GROUP.bit_length() - 1)
        lax.fori_loop(0, n_groups, scatter_group, 0)
        lax.fori_loop(n_groups * ROW_GROUP, n_valid, scatter_one, 0)

        @pl.when(i == n_used - 1)
        def _():
            wait_scatter(n_valid)


def _experts(xs, code, block_e, n_valid, n_used, w_gate, w_up, w_down, *, n_assign, layer, te):
    d, de = w_gate.shape[2:]
    nch = d // 2 // LANES
    assert nch % 8 == 0, "a token must cover whole (8, 128) tiles"
    n_blocks = block_e.shape[0]
    assert te % ROW_GROUP == 0 and code.shape[0] == n_blocks * te
    by_expert = lambda i, code, be, nv, nu: (layer, be[i], 0, 0)
    grid_spec = pltpu.PrefetchScalarGridSpec(
        num_scalar_prefetch=4,
        grid=(n_blocks,),
        in_specs=[pl.BlockSpec((te * nch, LANES),
                               lambda i, code, be, nv, nu: (jnp.minimum(i, nu[0] - 1), 0)),
                  pl.BlockSpec((None, None, d, de), by_expert),
                  pl.BlockSpec((None, None, d, de), by_expert),
                  pl.BlockSpec((None, None, de, d), by_expert)],
        out_specs=pl.BlockSpec(memory_space=pl.ANY),
        scratch_shapes=[pltpu.VMEM((te * nch, LANES), jnp.uint32),
                        pltpu.VMEM((d, de), BF16), pltpu.VMEM((d, de), BF16),
                        pltpu.VMEM((de, d), BF16),
                        pltpu.SemaphoreType.DMA(())],
    )
    return pl.pallas_call(
        _experts_body,
        out_shape=jax.ShapeDtypeStruct((n_assign * nch, LANES), jnp.uint32),
        grid_spec=grid_spec,
        compiler_params=_params("arbitrary"),
        name="moe_experts",
    )(code, block_e, n_valid, n_used, xs, w_gate, w_up, w_down)


def _dispatch(ri, counts, *, te):
    n = ri.shape[0]
    counts = counts[0, :N_EXPERTS]
    padded = (counts + te - 1) // te * te
    pend = jnp.cumsum(padded)
    pstart = pend - padded
    n_blocks = TOP_K * n // te + N_EXPERTS
    cap = n_blocks * te
    expert = ri[:, :TOP_K].T[:, :, None]
    first = jnp.sum(jnp.where(expert == jnp.arange(N_EXPERTS), pstart, 0), axis=-1)
    slot = (first + ri[:, TOP_K:2 * TOP_K].T).reshape(-1).astype(jnp.int32)
    block_start = jnp.arange(n_blocks, dtype=jnp.int32) * te
    block_e = jnp.minimum(jnp.sum(pend[None, :] <= block_start[:, None], axis=1),
                          N_EXPERTS - 1).astype(jnp.int32)
    n_valid = jnp.clip(pstart[block_e] + counts[block_e] - block_start, 0, te).astype(jnp.int32)
    n_used = (pend[-1:] // te).astype(jnp.int32)
    pad_start = jnp.concatenate([pstart + counts, pend[-1:]]).astype(jnp.int32)
    pad_len = jnp.concatenate([padded - counts, cap - pend[-1:]]).astype(jnp.int32)
    return slot, block_e, n_valid, n_used, pad_start, pad_len


def _combine_body(x_ref, m0_ref, m1_ref, rw_ref, g2_ref, o_ref):
    o_ref[...] = _moe_residual(x_ref[...], m0_ref, m1_ref, rw_ref, g2_ref)


def _combine(x, m, rw, gate2, *, seq, tm):
    n, d = x.shape
    nch = d // 2 // LANES
    per_b = seq // tm
    row_spec = pl.BlockSpec((tm, d), lambda i: (i, 0))
    return pl.pallas_call(
        _combine_body,
        out_shape=jax.ShapeDtypeStruct((n, d), F32),
        grid=(n // tm,),
        in_specs=[row_spec,
                  pl.BlockSpec((tm * nch, LANES), lambda i: (i, 0)),
                  pl.BlockSpec((tm * nch, LANES), lambda i: (n // tm + i, 0)),
                  pl.BlockSpec((tm, rw.shape[1]), lambda i: (i, 0)),
                  pl.BlockSpec((None, 1, d), lambda i: (i // per_b, 0, 0))],
        out_specs=row_spec,
        compiler_params=_params("arbitrary"),
        name="moe_combine",
    )(x, m, m, rw, gate2)


def _tiles(seq):
    tm = min(512, seq)
    tq = min(256, seq)
    te = 256
    return tm, tq, te


def kernel(x, c, w_mod, b_mod, mod_layer, norm1_g, w_in, gm_norm_g, gm_ws, gm_bs, q_norm_g, k_norm_g, out_norm_g, w_out, norm2_g, w_group, b_group, w_route, b_route, w_gate, w_up, w_down):
    batch, seq, d = x.shape
    depth = mod_layer.shape[0]
    n = batch * seq
    tm, tq, te = _tiles(seq)
    assert seq % tm == 0 and seq % tq == 0 and tm % LANES == 0 and tq % LANES == 0
    assert (TOP_K * n) % te == 0 and w_group.shape[2] == N_GROUPS and w_route.shape[2] == N_EXPERTS
    assert w_in.shape[2] * 2 == 5 * w_out.shape[1]

    mod_all = _modulation(c, w_mod, b_mod, mod_layer).reshape(depth, batch, N_MOD, 1, d)
    w_in_b, w_out_b = w_in.astype(BF16), w_out.astype(BF16)
    pad = LANES - N_GROUPS - N_EXPERTS
    w_router = jnp.pad(jnp.concatenate([w_group, w_route], axis=2), ((0, 0), (0, 0), (0, pad)))
    w_router_hi = w_router.astype(BF16)
    w_router_lo = (w_router - w_router_hi.astype(F32)).astype(BF16)
    w_router = jnp.concatenate([w_router_hi, w_router_lo], axis=2)
    b_router = jnp.pad(jnp.concatenate([b_group, b_route], axis=1), ((0, 0), (0, pad)))

    xf = x.reshape(n, d)
    moe = None
    for l in range(depth):
        xf, ya, q, k, v = _mix_in(xf, moe, mod_all[l], norm1_g[l], w_in_b[l], gm_norm_g[l], gm_ws[l],
                                  gm_bs[l], q_norm_g[l], k_norm_g[l], out_norm_g[l], seq=seq, tm=tm)
        yb = _stick_break(q, k, v, out_norm_g[l], batch=batch, seq=seq, tq=tq)
        xf, h2, ri, rw, counts = _mix_out(xf, ya, yb, mod_all[l], w_out_b[l], norm2_g[l],
                                          w_router[l], b_router[l:l + 1], seq=seq, tm=tm)
        slot, block_e, n_valid, n_used, pad_start, pad_len = _dispatch(ri, counts, te=te)
        xs, code = _send_rows(h2, slot, pad_start, pad_len, cap=block_e.shape[0] * te, tm=tm)
        m = _experts(xs, code, block_e, n_valid, n_used, w_gate, w_up, w_down,
                     n_assign=slot.shape[0], layer=l, te=te)
        moe = (m, rw, mod_all[l][:, 5])
    out = _combine(xf, *moe, seq=seq, tm=tm)
    return out.reshape(batch, seq, d)
```

```python
import functools

import jax
import jax.numpy as jnp
from jax import lax
from jax.experimental import pallas as pl
from jax.experimental.pallas import tpu as pltpu

F32 = jnp.float32
BF16 = jnp.bfloat16

EPS = 1e-6
LANES = 128
N_MOD = 6
N_GROUPS = 4
EXPERTS_PER_GROUP = 8
N_EXPERTS = N_GROUPS * EXPERTS_PER_GROUP
TOP_K = 2
LOG2E = 1.4426950408889634
LOG2_WEIGHT_FLOOR = -104.0 * LOG2E
Q_FOLD = -(LANES ** -0.5) * LOG2E
ROW_GROUP = 8
VMEM_LIMIT = 56 * 1024 * 1024
HIGH_HALF = 0xFFFF0000


def _rms(x):
    return x * lax.rsqrt(jnp.mean(x * x, axis=-1, keepdims=True) + EPS)


def _gelu(x):
    return 0.5 * x * (1.0 + lax.erf(x * (2.0 ** -0.5)))


def _pack_halves(x):
    half = x.shape[1] // 2
    bits = lax.bitcast_convert_type(x.astype(BF16).astype(F32), jnp.uint32)
    return (lax.shift_right_logical(bits[:, :half], jnp.uint32(16))
            | (bits[:, half:] & jnp.uint32(HIGH_HALF)))


def _unpack_halves(w):
    lo = lax.bitcast_convert_type(lax.shift_left(w, jnp.uint32(16)), F32)
    hi = lax.bitcast_convert_type(w & jnp.uint32(HIGH_HALF), F32)
    return lo, hi


def _store_token_major(ref, base, packed):
    rows, width = packed.shape
    nch = width // LANES
    for c in range(nch):
        ref[pl.ds(base + c, rows, stride=nch), :] = packed[:, c * LANES:(c + 1) * LANES]


def _load_token_major(ref, base, rows, nch):
    return jnp.concatenate([ref[pl.ds(base + c, rows, stride=nch), :] for c in range(nch)], axis=-1)


def _moe_residual(x, m0_ref, m1_ref, rw_ref, g2_ref, r0=0):
    rows, d = x.shape
    nch = d // 2 // LANES
    lo0, hi0 = _unpack_halves(_load_token_major(m0_ref, r0 * nch, rows, nch))
    lo1, hi1 = _unpack_halves(_load_token_major(m1_ref, r0 * nch, rows, nch))
    w1, w2 = rw_ref[r0:r0 + rows, 0:1], rw_ref[r0:r0 + rows, 1:2]
    m = jnp.concatenate([w1 * lo0 + w2 * lo1, w1 * hi0 + w2 * hi1], axis=-1)
    return x + g2_ref[...] * m


def _params(*sem):
    return pltpu.CompilerParams(dimension_semantics=sem, vmem_limit_bytes=VMEM_LIMIT)


def _mod_body(c_ref, w_ref, b_ref, ml_ref, o_ref):
    c = c_ref[...]
    sc = c * jax.nn.sigmoid(c)
    r = jnp.dot(sc, w_ref[...], preferred_element_type=F32,
                precision=lax.Precision.HIGHEST) + b_ref[...]
    for l in range(o_ref.shape[0]):
        o_ref[l] = r + ml_ref[l:l + 1, :]


def _modulation(c, w_mod, b_mod, mod_layer):
    b, d = c.shape
    depth, w = mod_layer.shape
    tn = 1024
    return pl.pallas_call(
        _mod_body,
        out_shape=jax.ShapeDtypeStruct((depth, b, w), F32),
        grid=(w // tn,),
        in_specs=[pl.BlockSpec((b, d), lambda j: (0, 0)),
                  pl.BlockSpec((d, tn), lambda j: (0, j)),
                  pl.BlockSpec((1, tn), lambda j: (0, j)),
                  pl.BlockSpec((depth, tn), lambda j: (0, j))],
        out_specs=pl.BlockSpec((depth, b, tn), lambda j: (0, 0, j)),
        compiler_params=_params("arbitrary"),
        name="modulation",
    )(c, w_mod, b_mod.reshape(1, w), mod_layer)


def _mix_in_body(has_moe, *refs):
    refs = list(refs)
    x_ref = refs.pop(0)
    if has_moe:
        m0_ref, m1_ref, rw_ref, g2_ref = refs.pop(0), refs.pop(0), refs.pop(0), refs.pop(0)
    (mod_ref, g1_ref, win_ref, gmg_ref, ws_ref, bs_ref, qg_ref, kg_ref, og_ref) = refs[:9]
    refs = refs[9:]
    if has_moe:
        xo_ref = refs.pop(0)
    ya_ref, q_ref, k_ref, v_ref = refs

    tm = x_ref.shape[0]
    dq = ya_ref.shape[1]
    shift1, scale1 = mod_ref[0], mod_ref[1]
    gain1 = g1_ref[...] * (1.0 + scale1)
    row = lax.broadcasted_iota(jnp.int32, (LANES, LANES), 0)
    col = lax.broadcasted_iota(jnp.int32, (LANES, LANES), 1)
    causal = row >= col

    def normed(r0, rows):
        x = x_ref[r0:r0 + rows, :]
        if has_moe:
            x = _moe_residual(x, m0_ref, m1_ref, rw_ref, g2_ref, r0)
            xo_ref[r0:r0 + rows, :] = x
        return (_rms(x) * gain1 + shift1).astype(BF16)

    def finish(r0, rows, proj):
        for g in range(dq // LANES):
            sl = slice(g * LANES, (g + 1) * LANES)
            u = _gelu(proj[:, g * LANES:(g + 1) * LANES])
            vg = _gelu(proj[:, dq + g * LANES:dq + (g + 1) * LANES])
            vg = (_rms(vg) * gmg_ref[:, sl]).astype(BF16)
            wg = jnp.where(causal, ws_ref[g], 0.0).astype(BF16)
            for c in range(rows // LANES):
                rs = slice(c * LANES, (c + 1) * LANES)
                s = jnp.dot(wg, vg[rs], preferred_element_type=F32) + bs_ref[g]
                ya = u[rs] * s
                ya_ref[r0 + c * LANES:r0 + (c + 1) * LANES, sl] = (
                    _rms(ya) * og_ref[:, sl]).astype(BF16)
        for hd in range(dq // LANES):
            sl = slice(hd * LANES, (hd + 1) * LANES)
            qh = proj[:, 2 * dq + hd * LANES:2 * dq + (hd + 1) * LANES]
            kh = proj[:, 3 * dq + hd * LANES:3 * dq + (hd + 1) * LANES]
            q_ref[r0:r0 + rows, sl] = (_rms(qh) * (qg_ref[...] * Q_FOLD)).astype(BF16)
            k_ref[r0:r0 + rows, sl] = (_rms(kh) * kg_ref[...]).astype(BF16)
        v_ref[r0:r0 + rows, :] = proj[:, 4 * dq:].astype(BF16)

    n_parts = 2 if tm % (2 * LANES) == 0 else 1
    rows = tm // n_parts
    hs = [normed(p * rows, rows) for p in range(n_parts)]
    projs = [jnp.dot(h, win_ref[...], preferred_element_type=F32) for h in hs]
    for p in range(n_parts):
        finish(p * rows, rows, projs[p])


def _mix_in(x, moe, mod, g1, w_in, gm_g, gm_ws, gm_bs, q_g, k_g, out_g, *, seq, tm):
    n, d = x.shape
    dq = w_in.shape[1] // 5
    ng = dq // LANES
    per_b = seq // tm
    row_spec = pl.BlockSpec((tm, d), lambda i: (i, 0))
    const2 = lambda i: (0, 0)
    in_specs = [row_spec]
    args = [x]
    if moe is not None:
        m, rw, gate2 = moe
        nch = d // 2 // LANES
        in_specs += [pl.BlockSpec((tm * nch, LANES), lambda i: (i, 0)),
                     pl.BlockSpec((tm * nch, LANES), lambda i: (n // tm + i, 0)),
                     pl.BlockSpec((tm, rw.shape[1]), lambda i: (i, 0)),
                     pl.BlockSpec((None, 1, d), lambda i: (i // per_b, 0, 0))]
        args += [m, m, rw, gate2]
    in_specs += [pl.BlockSpec((None, N_MOD, 1, d), lambda i: (i // per_b, 0, 0, 0)),
                 pl.BlockSpec((1, d), const2),
                 pl.BlockSpec(w_in.shape, const2),
                 pl.BlockSpec((1, dq), const2),
                 pl.BlockSpec((ng, LANES, LANES), lambda i: (0, 0, 0)),
                 pl.BlockSpec((ng, LANES, 1), lambda i: (0, 0, 0)),
                 pl.BlockSpec((1, LANES), const2),
                 pl.BlockSpec((1, LANES), const2),
                 pl.BlockSpec((1, dq), const2)]
    args += [mod, g1.reshape(1, d), w_in, gm_g.reshape(1, dq), gm_ws, gm_bs.reshape(ng, LANES, 1),
             q_g.reshape(1, LANES), k_g.reshape(1, LANES), out_g.reshape(1, -1)]
    act = jax.ShapeDtypeStruct((n, dq), BF16)
    act_spec = pl.BlockSpec((tm, dq), lambda i: (i, 0))
    out_shape = [act, act, act, act]
    out_specs = [act_spec, act_spec, act_spec, act_spec]
    if moe is not None:
        out_shape = [jax.ShapeDtypeStruct((n, d), F32)] + out_shape
        out_specs = [row_spec] + out_specs
    outs = pl.pallas_call(
        functools.partial(_mix_in_body, moe is not None),
        out_shape=out_shape, grid=(n // tm,), in_specs=in_specs, out_specs=out_specs,
        compiler_params=_params("arbitrary"), name="mix_in",
    )(*args)
    if moe is None:
        return (x,) + tuple(outs)
    return tuple(outs)


def _stick_break_body(q_ref, k_ref, v_ref, og_ref, o_ref, carry_ref, acc_ref):
    tq, dq = q_ref.shape
    nh = dq // LANES
    nsub = tq // LANES
    qi = pl.program_id(1)
    r2 = lax.broadcasted_iota(jnp.int32, (LANES, 2 * LANES), 0)
    c2 = lax.broadcasted_iota(jnp.int32, (LANES, 2 * LANES), 1)
    tri_ones = jnp.where((r2 > c2) | (c2 >= LANES), 1.0, 0.0).astype(BF16)

    carry_ref[...] = jnp.zeros_like(carry_ref)
    acc_ref[...] = jnp.zeros_like(acc_ref)

    def key_block(r0, kb, masked):
        rows = tq - r0
        heads = [slice(hd * LANES, (hd + 1) * LANES) for hd in range(nh)]
        start = pl.multiple_of(kb * LANES, LANES)
        if masked:
            rr = lax.broadcasted_iota(jnp.int32, (rows, LANES), 0)
            cc = lax.broadcasted_iota(jnp.int32, (rows, LANES), 1)
            keep = cc < rr
        nzs = [lax.dot_general(q_ref[r0:, hs], k_ref[pl.ds(start, LANES), hs],
                               (((1,), (1,)), ((), ())), preferred_element_type=F32)
               for hs in heads]
        lfs, wbs = [], []
        for nz in nzs:
            lf = jnp.minimum(nz, 0.0) - jnp.log2(1.0 + jnp.exp2(jnp.minimum(nz, -nz)))
            if masked:
                lf = jnp.where(keep, lf, 0.0)
            lfs.append(lf)
            wbs.append(jnp.dot(lf.astype(BF16), tri_ones, preferred_element_type=F32))
        for hd, hs in enumerate(heads):
            carry = carry_ref[hd, r0:, :]
            a = jnp.exp2(lfs[hd] - nzs[hd] + wbs[hd][:, :LANES] + carry)
            if masked:
                a = jnp.where(keep, a, 0.0)
            acc_ref[hd, r0:, :] += jnp.dot(a.astype(BF16), v_ref[pl.ds(start, LANES), hs],
                                           preferred_element_type=F32)
            carry_ref[hd, r0:, :] = carry + wbs[hd][:, LANES:]

    for j in reversed(range(nsub)):
        key_block(j * LANES, qi * nsub + j, True)

    def more(state):
        kb, live = state
        return jnp.logical_and(kb >= 0, live)

    def step(state):
        kb, _ = state
        key_block(0, kb, False)
        return kb - 1, jnp.max(carry_ref[...]) > LOG2_WEIGHT_FLOOR

    lax.while_loop(more, step, (qi * nsub - 1, jnp.max(carry_ref[...]) > LOG2_WEIGHT_FLOOR))
    for hd in range(nh):
        hs = slice(hd * LANES, (hd + 1) * LANES)
        o_ref[:, hs] = (_rms(acc_ref[hd]) * og_ref[:, hs]).astype(BF16)


def _stick_break(q, k, v, out_g, *, batch, seq, tq):
    n, dq = q.shape
    nh = dq // LANES
    per_b = seq // tq
    q_spec = pl.BlockSpec((tq, dq), lambda b, i: (b * per_b + i, 0))
    kv_spec = pl.BlockSpec((seq, dq), lambda b, i: (b, 0), pipeline_mode=pl.Buffered(1))
    return pl.pallas_call(
        _stick_break_body,
        out_shape=jax.ShapeDtypeStruct((n, dq), BF16),
        grid=(batch, per_b),
        in_specs=[q_spec, kv_spec, kv_spec, pl.BlockSpec((1, dq), lambda b, i: (0, 1))],
        out_specs=q_spec,
        scratch_shapes=[pltpu.VMEM((nh, tq, LANES), F32), pltpu.VMEM((nh, tq, LANES), F32)],
        compiler_params=_params("arbitrary", "arbitrary"),
        name="stick_break",
    )(q, k, v, out_g.reshape(1, -1))


def _mix_out_body(x_ref, ya_ref, yb_ref, mod_ref, woa_ref, wob_ref, g2_ref, wr_ref, br_ref,
                  x1_ref, h2_ref, ri_ref, rw_ref, cnt_ref, count_scr):
    tm = x_ref.shape[0]

    @pl.when(pl.program_id(0) == 0)
    def _():
        count_scr[...] = jnp.zeros_like(count_scr)

    gate1, shift2, scale2 = mod_ref[2], mod_ref[3], mod_ref[4]
    y = (jnp.dot(ya_ref[...], woa_ref[...], preferred_element_type=F32)
         + jnp.dot(yb_ref[...], wob_ref[...], preferred_element_type=F32))
    x1 = x_ref[...] + gate1 * y
    x1_ref[...] = x1
    h2 = _rms(x1) * g2_ref[...] * (1.0 + scale2) + shift2
    _store_token_major(h2_ref, 0, _pack_halves(h2))

    h_hi = h2.astype(BF16)
    h_lo = (h2 - h_hi.astype(F32)).astype(BF16)
    p = jnp.dot(h_hi, wr_ref[...], preferred_element_type=F32)
    lg = (p[:, :LANES] + p[:, LANES:]
          + jnp.dot(h_lo, wr_ref[:, :LANES], preferred_element_type=F32) + br_ref[...])
    lane = lax.broadcasted_iota(jnp.int32, (tm, LANES), 1)
    lane_f = lane.astype(F32)
    neg = -jnp.inf

    def first_max(vals):
        m = jnp.max(vals, axis=-1, keepdims=True)
        idx = jnp.min(jnp.where(vals == m, lane_f, float(LANES)), axis=-1, keepdims=True)
        return m, idx.astype(jnp.int32)

    is_group = lane < N_GROUPS
    gmax, gidx = first_max(jnp.where(is_group, lg, neg))
    p_group = 1.0 / jnp.sum(jnp.where(is_group, jnp.exp(lg - gmax), 0.0), axis=-1, keepdims=True)
    lo = N_GROUPS + gidx * EXPERTS_PER_GROUP
    el = jnp.where((lane >= lo) & (lane < lo + EXPERTS_PER_GROUP), lg, neg)
    m1, i1 = first_max(el)
    m2, i2 = first_max(jnp.where(lane == i1, neg, el))
    t = jnp.exp(m2 - m1)
    w1 = p_group / (1.0 + t)
    w2 = w1 * t
    e1 = i1 - N_GROUPS
    e2 = i2 - N_GROUPS

    hit1 = lane == e1
    hit2 = lane == e2
    onehot = jnp.where(hit1 | hit2, 1.0, 0.0)
    rr = lax.broadcasted_iota(jnp.int32, (tm, tm), 0)
    cc = lax.broadcasted_iota(jnp.int32, (tm, tm), 1)
    earlier = jnp.where(rr > cc, 1.0, 0.0).astype(BF16)
    base = jnp.dot(earlier, onehot.astype(BF16), preferred_element_type=F32) + count_scr[...]
    r1 = jnp.sum(jnp.where(hit1, base, 0.0), axis=-1, keepdims=True).astype(jnp.int32)
    r2 = jnp.sum(jnp.where(hit2, base, 0.0), axis=-1, keepdims=True).astype(jnp.int32)
    count_scr[...] += jnp.sum(onehot, axis=0, keepdims=True)
    cnt_ref[...] = count_scr[...].astype(jnp.int32)

    fields = jnp.where(lane == 0, e1, jnp.where(lane == 1, e2,
                                                jnp.where(lane == 2, r1,
                                                          jnp.where(lane == 3, r2, 0))))
    ri_ref[...] = fields.T[:ri_ref.shape[0], :]
    l8 = lax.broadcasted_iota(jnp.int32, rw_ref.shape, 1)
    rw_ref[...] = jnp.where(l8 == 0, w1, w2)


def _mix_out(x, ya, yb, mod, w_out, g2, w_router, b_router, *, seq, tm):
    n, d = x.shape
    dq = ya.shape[1]
    nch = d // 2 // LANES
    per_b = seq // tm
    row_spec = pl.BlockSpec((tm, d), lambda i: (i, 0))
    act_spec = pl.BlockSpec((tm, dq), lambda i: (i, 0))
    small_spec = pl.BlockSpec((tm, 8), lambda i: (i, 0))
    const2 = lambda i: (0, 0)
    return pl.pallas_call(
        _mix_out_body,
        out_shape=[jax.ShapeDtypeStruct((n, d), F32),
                   jax.ShapeDtypeStruct((n * nch, LANES), jnp.uint32),
                   jax.ShapeDtypeStruct((8, n), jnp.int32), jax.ShapeDtypeStruct((n, 8), F32),
                   jax.ShapeDtypeStruct((1, LANES), jnp.int32)],
        grid=(n // tm,),
        in_specs=[row_spec, act_spec, act_spec,
                  pl.BlockSpec((None, N_MOD, 1, d), lambda i: (i // per_b, 0, 0, 0)),
                  pl.BlockSpec((dq, d), const2),
                  pl.BlockSpec((dq, d), lambda i: (1, 0)),
                  pl.BlockSpec((1, d), const2),
                  pl.BlockSpec((d, 2 * LANES), const2),
                  pl.BlockSpec((1, LANES), const2)],
        out_specs=[row_spec, pl.BlockSpec((tm * nch, LANES), lambda i: (i, 0)),
                   pl.BlockSpec((8, tm), lambda i: (0, i)), small_spec,
                   pl.BlockSpec((1, LANES), const2)],
        scratch_shapes=[pltpu.VMEM((1, LANES), F32)],
        compiler_params=_params("arbitrary"),
        name="mix_out",
    )(x, ya, yb, mod, w_out, w_out, g2.reshape(1, d), w_router, b_router)


def _send_rows_body(slot_ref, ps_ref, pn_ref, h_ref, zeros_hbm, xs_hbm, code_hbm,
                    code_ref, zbuf, sem, zsem, csem):
    nch = zbuf.shape[0]
    tm = h_ref.shape[0] // nch
    i = pl.program_id(0)
    n_tok = slot_ref.shape[0] // TOP_K
    n_pad = xs_hbm.shape[0] // nch - slot_ref.shape[0]

    def rows(ref, index):
        return ref.at[pl.ds(pl.multiple_of(index * nch, nch), nch), :]

    @pl.when(i == 0)
    def _():
        fill_code = pltpu.make_async_copy(zeros_hbm, code_ref, csem)
        fill_code.start()
        fill_code.wait()
        zbuf[...] = jnp.zeros_like(zbuf)

        def region(e, carry):
            first = ps_ref[e]

            def fill(j, c):
                pltpu.make_async_copy(zbuf, rows(xs_hbm, first + j), zsem).start()
                return c

            return lax.fori_loop(0, pn_ref[e], fill, carry)

        lax.fori_loop(0, ps_ref.shape[0], region, 0)
        pltpu.make_async_copy(xs_hbm.at[pl.ds(0, n_pad * nch), :],
                              xs_hbm.at[pl.ds(0, n_pad * nch), :], zsem).wait()

    def group(j, carry):
        for u in range(ROW_GROUP):
            r = j * ROW_GROUP + u
            for k in range(TOP_K):
                a = k * n_tok + i * tm + r
                dst = slot_ref[a]
                code_ref[dst] = a
                pltpu.make_async_copy(rows(h_ref, r), rows(xs_hbm, dst), sem).start(priority=k)
        return carry

    lax.fori_loop(0, tm // ROW_GROUP, group, 0)
    pltpu.make_async_copy(xs_hbm.at[pl.ds(0, TOP_K * tm * nch), :],
                          xs_hbm.at[pl.ds(0, TOP_K * tm * nch), :], sem).wait()

    @pl.when(i == pl.num_programs(0) - 1)
    def _():
        emit_code = pltpu.make_async_copy(code_ref, code_hbm, csem)
        emit_code.start()
        emit_code.wait()


def _send_rows(h2, slot, pad_start, pad_len, *, cap, tm):
    nch = h2.shape[0] * TOP_K // slot.shape[0]
    n = h2.shape[0] // nch
    assert tm % ROW_GROUP == 0 and cap - slot.shape[0] > 0
    return pl.pallas_call(
        _send_rows_body,
        out_shape=[jax.ShapeDtypeStruct((cap * nch, LANES), jnp.uint32),
                   jax.ShapeDtypeStruct((cap,), jnp.int32)],
        grid_spec=pltpu.PrefetchScalarGridSpec(
            num_scalar_prefetch=3, grid=(n // tm,),
            in_specs=[pl.BlockSpec((tm * nch, LANES), lambda i, *_: (i, 0)),
                      pl.BlockSpec(memory_space=pl.ANY)],
            out_specs=[pl.BlockSpec(memory_space=pl.ANY), pl.BlockSpec(memory_space=pl.ANY)],
            scratch_shapes=[pltpu.SMEM((cap,), jnp.int32),
                            pltpu.VMEM((nch, LANES), jnp.uint32),
                            pltpu.SemaphoreType.DMA(()), pltpu.SemaphoreType.DMA(()),
                            pltpu.SemaphoreType.DMA(())]),
        compiler_params=_params("arbitrary"),
        name="moe_send_rows",
    )(slot, pad_start, pad_len, h2, jnp.zeros((cap,), jnp.int32))


def _experts_body(code_ref, be_ref, nv_ref, nu_ref, x_ref, wg_ref, wu_ref, wd_ref,
                  out_hbm, ybuf, wg_b, wu_b, wd_b, ssem):
    nch = wg_b.shape[0] // 2 // LANES
    te = ybuf.shape[0] // nch
    i = pl.program_id(0)
    n_used = nu_ref[0]

    def token_rows(ref, index):
        return ref.at[pl.ds(pl.multiple_of(index * nch, nch), nch), :]

    def wait_scatter(rows):
        for b in reversed(range(te.bit_length())):
            size = (1 << b) * nch

            @pl.when((rows & (1 << b)) != 0)
            def _():
                pltpu.make_async_copy(ybuf.at[pl.ds(0, size), :], out_hbm.at[pl.ds(0, size), :],
                                      ssem).wait()

    @pl.when(i < n_used)
    def _():
        base = i * te
        n_valid = nv_ref[i]

        @pl.when(jnp.logical_or(i == 0, be_ref[i] != be_ref[jnp.maximum(i - 1, 0)]))
        def _():
            wg_b[...] = wg_ref[...].astype(BF16)
            wu_b[...] = wu_ref[...].astype(BF16)
            wd_b[...] = wd_ref[...].astype(BF16)

        half = nch * LANES
        x_lo, x_hi = _unpack_halves(_load_token_major(x_ref, 0, te, nch))
        x_lo, x_hi = x_lo.astype(BF16), x_hi.astype(BF16)
        g = (jnp.dot(x_lo, wg_b[:half], preferred_element_type=F32)
             + jnp.dot(x_hi, wg_b[half:], preferred_element_type=F32))
        u = (jnp.dot(x_lo, wu_b[:half], preferred_element_type=F32)
             + jnp.dot(x_hi, wu_b[half:], preferred_element_type=F32))
        act = (g * jax.nn.sigmoid(g) * u).astype(BF16)

        @pl.when(i > 0)
        def _():
            wait_scatter(nv_ref[jnp.maximum(i - 1, 0)])

        _store_token_major(ybuf, 0, _pack_halves(jnp.dot(act, wd_b[...],
                                                          preferred_element_type=F32)))

        def scatter(r, priority):
            pltpu.make_async_copy(token_rows(ybuf, r), token_rows(out_hbm, code_ref[base + r]),
                                  ssem).start(priority=priority)

        def scatter_group(j, carry):
            for u in range(ROW_GROUP):
                scatter(j * ROW_GROUP + u, u % 2)
            return carry

        def scatter_one(r, carry):
            scatter(r, 0)
            return carry

        n_groups = lax.shift_right_logical(n_valid, ROW_GROUP.bit_length() - 1)
        lax.fori_loop(0, n_groups, scatter_group, 0)
        lax.fori_loop(n_groups * ROW_GROUP, n_valid, scatter_one, 0)

        @pl.when(i == n_used - 1)
        def _():
            wait_scatter(n_valid)


def _experts(xs, code, block_e, n_valid, n_used, w_gate, w_up, w_down, *, n_assign, layer, te):
    d, de = w_gate.shape[2:]
    nch = d // 2 // LANES
    assert nch % 8 == 0, "a token must cover whole (8, 128) tiles"
    n_blocks = block_e.shape[0]
    assert te % ROW_GROUP == 0 and code.shape[0] == n_blocks * te
    by_expert = lambda i, code, be, nv, nu: (layer, be[i], 0, 0)
    grid_spec = pltpu.PrefetchScalarGridSpec(
        num_scalar_prefetch=4,
        grid=(n_blocks,),
        in_specs=[pl.BlockSpec((te * nch, LANES),
                               lambda i, code, be, nv, nu: (jnp.minimum(i, nu[0] - 1), 0)),
                  pl.BlockSpec((None, None, d, de), by_expert),
                  pl.BlockSpec((None, None, d, de), by_expert),
                  pl.BlockSpec((None, None, de, d), by_expert)],
        out_specs=pl.BlockSpec(memory_space=pl.ANY),
        scratch_shapes=[pltpu.VMEM((te * nch, LANES), jnp.uint32),
                        pltpu.VMEM((d, de), BF16), pltpu.VMEM((d, de), BF16),
                        pltpu.VMEM((de, d), BF16),
                        pltpu.SemaphoreType.DMA(())],
    )
    return pl.pallas_call(
        _experts_body,
        out_shape=jax.ShapeDtypeStruct((n_assign * nch, LANES), jnp.uint32),
        grid_spec=grid_spec,
        compiler_params=_params("arbitrary"),
        name="moe_experts",
    )(code, block_e, n_valid, n_used, xs, w_gate, w_up, w_down)


def _dispatch(ri, counts, *, te):
    n = ri.shape[1]
    counts = counts[0, :N_EXPERTS]
    padded = (counts + te - 1) // te * te
    pend = jnp.cumsum(padded)
    pstart = pend - padded
    n_blocks = TOP_K * n // te + N_EXPERTS
    cap = n_blocks * te
    expert, rank = ri[:TOP_K], ri[TOP_K:2 * TOP_K]
    first = jnp.sum(jnp.where(expert[None] == jnp.arange(N_EXPERTS)[:, None, None],
                              pstart[:, None, None], 0), axis=0)
    slot = (first + rank).reshape(-1).astype(jnp.int32)
    block_start = jnp.arange(n_blocks, dtype=jnp.int32) * te
    block_e = jnp.minimum(jnp.sum(pend[None, :] <= block_start[:, None], axis=1),
                          N_EXPERTS - 1).astype(jnp.int32)
    n_valid = jnp.clip(pstart[block_e] + counts[block_e] - block_start, 0, te).astype(jnp.int32)
    n_used = (pend[-1:] // te).astype(jnp.int32)
    pad_start = jnp.concatenate([pstart + counts, pend[-1:]]).astype(jnp.int32)
    pad_len = jnp.concatenate([padded - counts, cap - pend[-1:]]).astype(jnp.int32)
    return slot, block_e, n_valid, n_used, pad_start, pad_len


def _combine_body(x_ref, m0_ref, m1_ref, rw_ref, g2_ref, o_ref):
    o_ref[...] = _moe_residual(x_ref[...], m0_ref, m1_ref, rw_ref, g2_ref)


def _combine(x, m, rw, gate2, *, seq, tm):
    n, d = x.shape
    nch = d // 2 // LANES
    per_b = seq // tm
    row_spec = pl.BlockSpec((tm, d), lambda i: (i, 0))
    return pl.pallas_call(
        _combine_body,
        out_shape=jax.ShapeDtypeStruct((n, d), F32),
        grid=(n // tm,),
        in_specs=[row_spec,
                  pl.BlockSpec((tm * nch, LANES), lambda i: (i, 0)),
                  pl.BlockSpec((tm * nch, LANES), lambda i: (n // tm + i, 0)),
                  pl.BlockSpec((tm, rw.shape[1]), lambda i: (i, 0)),
                  pl.BlockSpec((None, 1, d), lambda i: (i // per_b, 0, 0))],
        out_specs=row_spec,
        compiler_params=_params("arbitrary"),
        name="moe_combine",
    )(x, m, m, rw, gate2)


def _tiles(seq):
    tm = min(512, seq)
    tq = min(256, seq)
    te = 256
    return tm, tq, te


def kernel(x, c, w_mod, b_mod, mod_layer, norm1_g, w_in, gm_norm_g, gm_ws, gm_bs, q_norm_g, k_norm_g, out_norm_g, w_out, norm2_g, w_group, b_group, w_route, b_route, w_gate, w_up, w_down):
    batch, seq, d = x.shape
    depth = mod_layer.shape[0]
    n = batch * seq
    tm, tq, te = _tiles(seq)
    assert seq % tm == 0 and seq % tq == 0 and tm % LANES == 0 and tq % LANES == 0
    assert (TOP_K * n) % te == 0 and w_group.shape[2] == N_GROUPS and w_route.shape[2] == N_EXPERTS
    assert w_in.shape[2] * 2 == 5 * w_out.shape[1]

    mod_all = _modulation(c, w_mod, b_mod, mod_layer).reshape(depth, batch, N_MOD, 1, d)
    w_in_b, w_out_b = w_in.astype(BF16), w_out.astype(BF16)
    pad = LANES - N_GROUPS - N_EXPERTS
    w_router = jnp.pad(jnp.concatenate([w_group, w_route], axis=2), ((0, 0), (0, 0), (0, pad)))
    w_router_hi = w_router.astype(BF16)
    w_router_lo = (w_router - w_router_hi.astype(F32)).astype(BF16)
    w_router = jnp.concatenate([w_router_hi, w_router_lo], axis=2)
    b_router = jnp.pad(jnp.concatenate([b_group, b_route], axis=1), ((0, 0), (0, pad)))

    xf = x.reshape(n, d)
    moe = None
    for l in range(depth):
        xf, ya, q, k, v = _mix_in(xf, moe, mod_all[l], norm1_g[l], w_in_b[l], gm_norm_g[l], gm_ws[l],
                                  gm_bs[l], q_norm_g[l], k_norm_g[l], out_norm_g[l], seq=seq, tm=tm)
        yb = _stick_break(q, k, v, out_norm_g[l], batch=batch, seq=seq, tq=tq)
        xf, h2, ri, rw, counts = _mix_out(xf, ya, yb, mod_all[l], w_out_b[l], norm2_g[l],
                                          w_router[l], b_router[l:l + 1], seq=seq, tm=tm)
        slot, block_e, n_valid, n_used, pad_start, pad_len = _dispatch(ri, counts, te=te)
        xs, code = _send_rows(h2, slot, pad_start, pad_len, cap=block_e.shape[0] * te,
                              tm=min(4 * tm, seq))
        m = _experts(xs, code, block_e, n_valid, n_used, w_gate, w_up, w_down,
                     n_assign=slot.shape[0], layer=l, te=te)
        moe = (m, rw, mod_all[l][:, 5])
    out = _combine(xf, *moe, seq=seq, tm=tm)
    return out.reshape(batch, seq, d)
```

```python
import functools

import jax
import jax.numpy as jnp
from jax import lax
from jax.experimental import pallas as pl
from jax.experimental.pallas import tpu as pltpu

F32 = jnp.float32
BF16 = jnp.bfloat16

EPS = 1e-6
LANES = 128
N_MOD = 6
N_GROUPS = 4
EXPERTS_PER_GROUP = 8
N_EXPERTS = N_GROUPS * EXPERTS_PER_GROUP
TOP_K = 2
LOG2E = 1.4426950408889634
LOG2_WEIGHT_FLOOR = -104.0 * LOG2E
Q_FOLD = -(LANES ** -0.5) * LOG2E
ROW_GROUP = 8
VMEM_LIMIT = 56 * 1024 * 1024
HIGH_HALF = 0xFFFF0000


def _rms(x):
    return x * lax.rsqrt(jnp.mean(x * x, axis=-1, keepdims=True) + EPS)


def _gelu(x):
    return 0.5 * x * (1.0 + lax.erf(x * (2.0 ** -0.5)))


def _pack_halves(x):
    half = x.shape[1] // 2
    bits = lax.bitcast_convert_type(x.astype(BF16).astype(F32), jnp.uint32)
    return (lax.shift_right_logical(bits[:, :half], jnp.uint32(16))
            | (bits[:, half:] & jnp.uint32(HIGH_HALF)))


def _unpack_halves(w):
    lo = lax.bitcast_convert_type(lax.shift_left(w, jnp.uint32(16)), F32)
    hi = lax.bitcast_convert_type(w & jnp.uint32(HIGH_HALF), F32)
    return lo, hi


def _store_token_major(ref, base, packed):
    rows, width = packed.shape
    nch = width // LANES
    for c in range(nch):
        ref[pl.ds(base + c, rows, stride=nch), :] = packed[:, c * LANES:(c + 1) * LANES]


def _load_token_major(ref, base, rows, nch):
    return jnp.concatenate([ref[pl.ds(base + c, rows, stride=nch), :] for c in range(nch)], axis=-1)


def _moe_residual(x, m0_ref, m1_ref, rw_ref, g2_ref, r0=0):
    rows, d = x.shape
    nch = d // 2 // LANES
    lo0, hi0 = _unpack_halves(_load_token_major(m0_ref, r0 * nch, rows, nch))
    lo1, hi1 = _unpack_halves(_load_token_major(m1_ref, r0 * nch, rows, nch))
    w1, w2 = rw_ref[r0:r0 + rows, 0:1], rw_ref[r0:r0 + rows, 1:2]
    m = jnp.concatenate([w1 * lo0 + w2 * lo1, w1 * hi0 + w2 * hi1], axis=-1)
    return x + g2_ref[...] * m


def _params(*sem):
    return pltpu.CompilerParams(dimension_semantics=sem, vmem_limit_bytes=VMEM_LIMIT)


def _mod_body(c_ref, w_ref, b_ref, ml_ref, o_ref):
    c = c_ref[...]
    sc = c * jax.nn.sigmoid(c)
    r = jnp.dot(sc, w_ref[...], preferred_element_type=F32,
                precision=lax.Precision.HIGHEST) + b_ref[...]
    for l in range(o_ref.shape[0]):
        o_ref[l] = r + ml_ref[l:l + 1, :]


def _modulation(c, w_mod, b_mod, mod_layer):
    b, d = c.shape
    depth, w = mod_layer.shape
    tn = 1024
    return pl.pallas_call(
        _mod_body,
        out_shape=jax.ShapeDtypeStruct((depth, b, w), F32),
        grid=(w // tn,),
        in_specs=[pl.BlockSpec((b, d), lambda j: (0, 0)),
                  pl.BlockSpec((d, tn), lambda j: (0, j)),
                  pl.BlockSpec((1, tn), lambda j: (0, j)),
                  pl.BlockSpec((depth, tn), lambda j: (0, j))],
        out_specs=pl.BlockSpec((depth, b, tn), lambda j: (0, 0, j)),
        compiler_params=_params("arbitrary"),
        name="modulation",
    )(c, w_mod, b_mod.reshape(1, w), mod_layer)


def _mix_in_body(has_moe, *refs):
    refs = list(refs)
    x_ref = refs.pop(0)
    if has_moe:
        m0_ref, m1_ref, rw_ref, g2_ref = refs.pop(0), refs.pop(0), refs.pop(0), refs.pop(0)
    (mod_ref, g1_ref, win_ref, gmg_ref, ws_ref, bs_ref, qg_ref, kg_ref, og_ref) = refs[:9]
    refs = refs[9:]
    if has_moe:
        xo_ref = refs.pop(0)
    ya_ref, q_ref, k_ref, v_ref = refs

    tm = x_ref.shape[0]
    dq = ya_ref.shape[1]
    shift1, scale1 = mod_ref[0], mod_ref[1]
    gain1 = g1_ref[...] * (1.0 + scale1)
    row = lax.broadcasted_iota(jnp.int32, (LANES, LANES), 0)
    col = lax.broadcasted_iota(jnp.int32, (LANES, LANES), 1)
    causal = row >= col

    def normed(r0, rows):
        x = x_ref[r0:r0 + rows, :]
        if has_moe:
            x = _moe_residual(x, m0_ref, m1_ref, rw_ref, g2_ref, r0)
            xo_ref[r0:r0 + rows, :] = x
        return (_rms(x) * gain1 + shift1).astype(BF16)

    def finish(r0, rows, proj):
        for g in range(dq // LANES):
            sl = slice(g * LANES, (g + 1) * LANES)
            u = _gelu(proj[:, g * LANES:(g + 1) * LANES])
            vg = _gelu(proj[:, dq + g * LANES:dq + (g + 1) * LANES])
            vg = (_rms(vg) * gmg_ref[:, sl]).astype(BF16)
            wg = jnp.where(causal, ws_ref[g], 0.0).astype(BF16)
            for c in range(rows // LANES):
                rs = slice(c * LANES, (c + 1) * LANES)
                s = jnp.dot(wg, vg[rs], preferred_element_type=F32) + bs_ref[g]
                ya = u[rs] * s
                ya_ref[r0 + c * LANES:r0 + (c + 1) * LANES, sl] = (
                    _rms(ya) * og_ref[:, sl]).astype(BF16)
        for hd in range(dq // LANES):
            sl = slice(hd * LANES, (hd + 1) * LANES)
            qh = proj[:, 2 * dq + hd * LANES:2 * dq + (hd + 1) * LANES]
            kh = proj[:, 3 * dq + hd * LANES:3 * dq + (hd + 1) * LANES]
            q_ref[r0:r0 + rows, sl] = (_rms(qh) * (qg_ref[...] * Q_FOLD)).astype(BF16)
            k_ref[r0:r0 + rows, sl] = (_rms(kh) * kg_ref[...]).astype(BF16)
        v_ref[r0:r0 + rows, :] = proj[:, 4 * dq:].astype(BF16)

    n_parts = 2 if tm % (2 * LANES) == 0 else 1
    rows = tm // n_parts
    hs = [normed(p * rows, rows) for p in range(n_parts)]
    projs = [jnp.dot(h, win_ref[...], preferred_element_type=F32) for h in hs]
    for p in range(n_parts):
        finish(p * rows, rows, projs[p])


def _mix_in(x, moe, mod, g1, w_in, gm_g, gm_ws, gm_bs, q_g, k_g, out_g, *, seq, tm):
    n, d = x.shape
    dq = w_in.shape[1] // 5
    ng = dq // LANES
    per_b = seq // tm
    row_spec = pl.BlockSpec((tm, d), lambda i: (i, 0))
    const2 = lambda i: (0, 0)
    in_specs = [row_spec]
    args = [x]
    if moe is not None:
        m, rw, gate2 = moe
        nch = d // 2 // LANES
        in_specs += [pl.BlockSpec((tm * nch, LANES), lambda i: (i, 0)),
                     pl.BlockSpec((tm * nch, LANES), lambda i: (n // tm + i, 0)),
                     pl.BlockSpec((tm, rw.shape[1]), lambda i: (i, 0)),
                     pl.BlockSpec((None, 1, d), lambda i: (i // per_b, 0, 0))]
        args += [m, m, rw, gate2]
    in_specs += [pl.BlockSpec((None, N_MOD, 1, d), lambda i: (i // per_b, 0, 0, 0)),
                 pl.BlockSpec((1, d), const2),
                 pl.BlockSpec(w_in.shape, const2),
                 pl.BlockSpec((1, dq), const2),
                 pl.BlockSpec((ng, LANES, LANES), lambda i: (0, 0, 0)),
                 pl.BlockSpec((ng, LANES, 1), lambda i: (0, 0, 0)),
                 pl.BlockSpec((1, LANES), const2),
                 pl.BlockSpec((1, LANES), const2),
                 pl.BlockSpec((1, dq), const2)]
    args += [mod, g1.reshape(1, d), w_in, gm_g.reshape(1, dq), gm_ws, gm_bs.reshape(ng, LANES, 1),
             q_g.reshape(1, LANES), k_g.reshape(1, LANES), out_g.reshape(1, -1)]
    act = jax.ShapeDtypeStruct((n, dq), BF16)
    act_spec = pl.BlockSpec((tm, dq), lambda i: (i, 0))
    out_shape = [act, act, act, act]
    out_specs = [act_spec, act_spec, act_spec, act_spec]
    if moe is not None:
        out_shape = [jax.ShapeDtypeStruct((n, d), F32)] + out_shape
        out_specs = [row_spec] + out_specs
    outs = pl.pallas_call(
        functools.partial(_mix_in_body, moe is not None),
        out_shape=out_shape, grid=(n // tm,), in_specs=in_specs, out_specs=out_specs,
        compiler_params=_params("arbitrary"), name="mix_in",
    )(*args)
    if moe is None:
        return (x,) + tuple(outs)
    return tuple(outs)


def _stick_break_body(q_ref, k_ref, v_ref, og_ref, o_ref, carry_ref, acc_ref):
    tq, dq = q_ref.shape
    nh = dq // LANES
    nsub = tq // LANES
    qi = pl.program_id(1)
    r2 = lax.broadcasted_iota(jnp.int32, (LANES, 2 * LANES), 0)
    c2 = lax.broadcasted_iota(jnp.int32, (LANES, 2 * LANES), 1)
    tri_ones = jnp.where((r2 > c2) | (c2 >= LANES), 1.0, 0.0).astype(BF16)

    carry_ref[...] = jnp.zeros_like(carry_ref)
    acc_ref[...] = jnp.zeros_like(acc_ref)

    def key_block(r0, kb, masked):
        rows = tq - r0
        heads = [slice(hd * LANES, (hd + 1) * LANES) for hd in range(nh)]
        start = pl.multiple_of(kb * LANES, LANES)
        if masked:
            rr = lax.broadcasted_iota(jnp.int32, (rows, LANES), 0)
            cc = lax.broadcasted_iota(jnp.int32, (rows, LANES), 1)
            keep = cc < rr
        nzs = [lax.dot_general(q_ref[r0:, hs], k_ref[pl.ds(start, LANES), hs],
                               (((1,), (1,)), ((), ())), preferred_element_type=F32)
               for hs in heads]
        lfs, wbs = [], []
        for nz in nzs:
            lf = jnp.minimum(nz, 0.0) - jnp.log2(1.0 + jnp.exp2(jnp.minimum(nz, -nz)))
            if masked:
                lf = jnp.where(keep, lf, 0.0)
            lfs.append(lf)
            wbs.append(jnp.dot(lf.astype(BF16), tri_ones, preferred_element_type=F32))
        for hd, hs in enumerate(heads):
            carry = carry_ref[hd, r0:, :]
            a = jnp.exp2(lfs[hd] - nzs[hd] + wbs[hd][:, :LANES] + carry)
            if masked:
                a = jnp.where(keep, a, 0.0)
            acc_ref[hd, r0:, :] += jnp.dot(a.astype(BF16), v_ref[pl.ds(start, LANES), hs],
                                           preferred_element_type=F32)
            carry_ref[hd, r0:, :] = carry + wbs[hd][:, LANES:]

    for j in reversed(range(nsub)):
        key_block(j * LANES, qi * nsub + j, True)

    def more(state):
        kb, live = state
        return jnp.logical_and(kb >= 0, live)

    def step(state):
        kb, _ = state
        key_block(0, kb, False)
        return kb - 1, jnp.max(carry_ref[...]) > LOG2_WEIGHT_FLOOR

    lax.while_loop(more, step, (qi * nsub - 1, jnp.max(carry_ref[...]) > LOG2_WEIGHT_FLOOR))
    for hd in range(nh):
        hs = slice(hd * LANES, (hd + 1) * LANES)
        o_ref[:, hs] = (_rms(acc_ref[hd]) * og_ref[:, hs]).astype(BF16)


def _stick_break(q, k, v, out_g, *, batch, seq, tq):
    n, dq = q.shape
    nh = dq // LANES
    per_b = seq // tq
    q_spec = pl.BlockSpec((tq, dq), lambda b, i: (b * per_b + i, 0))
    kv_spec = pl.BlockSpec((seq, dq), lambda b, i: (b, 0), pipeline_mode=pl.Buffered(1))
    return pl.pallas_call(
        _stick_break_body,
        out_shape=jax.ShapeDtypeStruct((n, dq), BF16),
        grid=(batch, per_b),
        in_specs=[q_spec, kv_spec, kv_spec, pl.BlockSpec((1, dq), lambda b, i: (0, 1))],
        out_specs=q_spec,
        scratch_shapes=[pltpu.VMEM((nh, tq, LANES), F32), pltpu.VMEM((nh, tq, LANES), F32)],
        compiler_params=_params("arbitrary", "arbitrary"),
        name="stick_break",
    )(q, k, v, out_g.reshape(1, -1))


def _mix_out_body(x_ref, ya_ref, yb_ref, mod_ref, woa_ref, wob_ref, g2_ref, wr_ref, br_ref,
                  x1_ref, h2_ref, ri_ref, rw_ref, cnt_ref, count_scr):
    tm = x_ref.shape[0]

    @pl.when(pl.program_id(0) == 0)
    def _():
        count_scr[...] = jnp.zeros_like(count_scr)

    gate1, shift2, scale2 = mod_ref[2], mod_ref[3], mod_ref[4]
    y = (jnp.dot(ya_ref[...], woa_ref[...], preferred_element_type=F32)
         + jnp.dot(yb_ref[...], wob_ref[...], preferred_element_type=F32))
    x1 = x_ref[...] + gate1 * y
    x1_ref[...] = x1
    h2 = _rms(x1) * g2_ref[...] * (1.0 + scale2) + shift2
    _store_token_major(h2_ref, 0, _pack_halves(h2))

    h_hi = h2.astype(BF16)
    h_lo = (h2 - h_hi.astype(F32)).astype(BF16)
    p = jnp.dot(h_hi, wr_ref[...], preferred_element_type=F32)
    lg = (p[:, :LANES] + p[:, LANES:]
          + jnp.dot(h_lo, wr_ref[:, :LANES], preferred_element_type=F32) + br_ref[...])
    lane = lax.broadcasted_iota(jnp.int32, (tm, LANES), 1)
    lane_f = lane.astype(F32)
    neg = -jnp.inf

    def first_max(vals):
        m = jnp.max(vals, axis=-1, keepdims=True)
        idx = jnp.min(jnp.where(vals == m, lane_f, float(LANES)), axis=-1, keepdims=True)
        return m, idx.astype(jnp.int32)

    is_group = lane < N_GROUPS
    gmax, gidx = first_max(jnp.where(is_group, lg, neg))
    p_group = 1.0 / jnp.sum(jnp.where(is_group, jnp.exp(lg - gmax), 0.0), axis=-1, keepdims=True)
    lo = N_GROUPS + gidx * EXPERTS_PER_GROUP
    el = jnp.where((lane >= lo) & (lane < lo + EXPERTS_PER_GROUP), lg, neg)
    m1, i1 = first_max(el)
    m2, i2 = first_max(jnp.where(lane == i1, neg, el))
    t = jnp.exp(m2 - m1)
    w1 = p_group / (1.0 + t)
    w2 = w1 * t
    e1 = i1 - N_GROUPS
    e2 = i2 - N_GROUPS

    hit1 = lane == e1
    hit2 = lane == e2
    onehot = jnp.where(hit1 | hit2, 1.0, 0.0)
    rr = lax.broadcasted_iota(jnp.int32, (tm, tm), 0)
    cc = lax.broadcasted_iota(jnp.int32, (tm, tm), 1)
    earlier = jnp.where(rr > cc, 1.0, 0.0).astype(BF16)
    base = jnp.dot(earlier, onehot.astype(BF16), preferred_element_type=F32) + count_scr[...]
    r1 = jnp.sum(jnp.where(hit1, base, 0.0), axis=-1, keepdims=True).astype(jnp.int32)
    r2 = jnp.sum(jnp.where(hit2, base, 0.0), axis=-1, keepdims=True).astype(jnp.int32)
    count_scr[...] += jnp.sum(onehot, axis=0, keepdims=True)
    cnt_ref[...] = count_scr[...].astype(jnp.int32)

    fields = jnp.where(lane == 0, e1, jnp.where(lane == 1, e2,
                                                jnp.where(lane == 2, r1,
                                                          jnp.where(lane == 3, r2, 0))))
    ri_ref[...] = fields.T[:ri_ref.shape[0], :]
    l8 = lax.broadcasted_iota(jnp.int32, rw_ref.shape, 1)
    rw_ref[...] = jnp.where(l8 == 0, w1, w2)


def _mix_out(x, ya, yb, mod, w_out, g2, w_router, b_router, *, seq, tm):
    n, d = x.shape
    dq = ya.shape[1]
    nch = d // 2 // LANES
    per_b = seq // tm
    row_spec = pl.BlockSpec((tm, d), lambda i: (i, 0))
    act_spec = pl.BlockSpec((tm, dq), lambda i: (i, 0))
    small_spec = pl.BlockSpec((tm, 8), lambda i: (i, 0))
    const2 = lambda i: (0, 0)
    return pl.pallas_call(
        _mix_out_body,
        out_shape=[jax.ShapeDtypeStruct((n, d), F32),
                   jax.ShapeDtypeStruct((n * nch, LANES), jnp.uint32),
                   jax.ShapeDtypeStruct((8, n), jnp.int32), jax.ShapeDtypeStruct((n, 8), F32),
                   jax.ShapeDtypeStruct((1, LANES), jnp.int32)],
        grid=(n // tm,),
        in_specs=[row_spec, act_spec, act_spec,
                  pl.BlockSpec((None, N_MOD, 1, d), lambda i: (i // per_b, 0, 0, 0)),
                  pl.BlockSpec((dq, d), const2),
                  pl.BlockSpec((dq, d), lambda i: (1, 0)),
                  pl.BlockSpec((1, d), const2),
                  pl.BlockSpec((d, 2 * LANES), const2),
                  pl.BlockSpec((1, LANES), const2)],
        out_specs=[row_spec, pl.BlockSpec((tm * nch, LANES), lambda i: (i, 0)),
                   pl.BlockSpec((8, tm), lambda i: (0, i)), small_spec,
                   pl.BlockSpec((1, LANES), const2)],
        scratch_shapes=[pltpu.VMEM((1, LANES), F32)],
        compiler_params=_params("arbitrary"),
        name="mix_out",
    )(x, ya, yb, mod, w_out, w_out, g2.reshape(1, d), w_router, b_router)


def _send_rows_body(slot_ref, ps_ref, pn_ref, h_ref, pad_code_hbm, xs_hbm, code_hbm,
                    code_ref, zbuf, sem, zsem, csem):
    nch = zbuf.shape[0]
    tm = h_ref.shape[0] // nch
    i = pl.program_id(0)
    n_tok = slot_ref.shape[0] // TOP_K
    n_pad = xs_hbm.shape[0] // nch - slot_ref.shape[0]

    def rows(ref, index):
        return ref.at[pl.ds(pl.multiple_of(index * nch, nch), nch), :]

    @pl.when(i == 0)
    def _():
        fill_code = pltpu.make_async_copy(pad_code_hbm, code_ref, csem)
        fill_code.start()
        fill_code.wait()
        zbuf[...] = jnp.zeros_like(zbuf)

        def region(e, carry):
            first = ps_ref[e]

            def fill(j, c):
                pltpu.make_async_copy(zbuf, rows(xs_hbm, first + j), zsem).start()
                return c

            return lax.fori_loop(0, pn_ref[e], fill, carry)

        lax.fori_loop(0, ps_ref.shape[0], region, 0)
        pltpu.make_async_copy(xs_hbm.at[pl.ds(0, n_pad * nch), :],
                              xs_hbm.at[pl.ds(0, n_pad * nch), :], zsem).wait()

    def group(j, carry):
        for u in range(ROW_GROUP):
            r = j * ROW_GROUP + u
            for k in range(TOP_K):
                a = k * n_tok + i * tm + r
                dst = slot_ref[a]
                code_ref[dst] = a
                pltpu.make_async_copy(rows(h_ref, r), rows(xs_hbm, dst), sem).start(priority=k)
        return carry

    lax.fori_loop(0, tm // ROW_GROUP, group, 0)
    pltpu.make_async_copy(xs_hbm.at[pl.ds(0, TOP_K * tm * nch), :],
                          xs_hbm.at[pl.ds(0, TOP_K * tm * nch), :], sem).wait()

    @pl.when(i == pl.num_programs(0) - 1)
    def _():
        emit_code = pltpu.make_async_copy(code_ref, code_hbm, csem)
        emit_code.start()
        emit_code.wait()


def _send_rows(h2, slot, pad_start, pad_len, pad_code, *, tm):
    cap = pad_code.shape[0]
    nch = h2.shape[0] * TOP_K // slot.shape[0]
    n = h2.shape[0] // nch
    assert tm % ROW_GROUP == 0 and cap - slot.shape[0] > 0
    return pl.pallas_call(
        _send_rows_body,
        out_shape=[jax.ShapeDtypeStruct((cap * nch, LANES), jnp.uint32),
                   jax.ShapeDtypeStruct((cap,), jnp.int32)],
        grid_spec=pltpu.PrefetchScalarGridSpec(
            num_scalar_prefetch=3, grid=(n // tm,),
            in_specs=[pl.BlockSpec((tm * nch, LANES), lambda i, *_: (i, 0)),
                      pl.BlockSpec(memory_space=pl.ANY)],
            out_specs=[pl.BlockSpec(memory_space=pl.ANY), pl.BlockSpec(memory_space=pl.ANY)],
            scratch_shapes=[pltpu.SMEM((cap,), jnp.int32),
                            pltpu.VMEM((nch, LANES), jnp.uint32),
                            pltpu.SemaphoreType.DMA(()), pltpu.SemaphoreType.DMA(()),
                            pltpu.SemaphoreType.DMA(())]),
        compiler_params=_params("arbitrary"),
        name="moe_send_rows",
    )(slot, pad_start, pad_len, h2, pad_code)


def _experts_body(code_ref, be_ref, nu_ref, x_ref, wg_ref, wu_ref, wd_ref,
                  out_hbm, ybuf, wg_b, wu_b, wd_b, ssem):
    nch = wg_b.shape[0] // 2 // LANES
    te = ybuf.shape[0] // nch
    spare = out_hbm.shape[0] - te * nch
    i = pl.program_id(0)
    n_used = nu_ref[0]

    def wait_scatter():
        pltpu.make_async_copy(ybuf, out_hbm.at[pl.ds(0, te * nch), :], ssem).wait()

    @pl.when(i == 0)
    def _():
        ybuf[...] = jnp.zeros_like(ybuf)
        pltpu.make_async_copy(ybuf, out_hbm.at[pl.ds(spare, te * nch), :], ssem).start()
        wait_scatter()

    @pl.when(i < n_used)
    def _():
        base = i * te

        @pl.when(jnp.logical_or(i == 0, be_ref[i] != be_ref[jnp.maximum(i - 1, 0)]))
        def _():
            wg_b[...] = wg_ref[...].astype(BF16)
            wu_b[...] = wu_ref[...].astype(BF16)
            wd_b[...] = wd_ref[...].astype(BF16)

        half = nch * LANES
        x_lo, x_hi = _unpack_halves(_load_token_major(x_ref, 0, te, nch))
        x_lo, x_hi = x_lo.astype(BF16), x_hi.astype(BF16)
        g = (jnp.dot(x_lo, wg_b[:half], preferred_element_type=F32)
             + jnp.dot(x_hi, wg_b[half:], preferred_element_type=F32))
        u = (jnp.dot(x_lo, wu_b[:half], preferred_element_type=F32)
             + jnp.dot(x_hi, wu_b[half:], preferred_element_type=F32))
        act = (g * jax.nn.sigmoid(g) * u).astype(BF16)

        @pl.when(i > 0)
        def _():
            wait_scatter()

        _store_token_major(ybuf, 0, _pack_halves(jnp.dot(act, wd_b[...],
                                                          preferred_element_type=F32)))
        for r in range(te):
            dst = pl.multiple_of(code_ref[base + r] * nch, nch)
            pltpu.make_async_copy(ybuf.at[pl.ds(r * nch, nch), :], out_hbm.at[pl.ds(dst, nch), :],
                                  ssem).start(priority=r % 2)

        @pl.when(i == n_used - 1)
        def _():
            wait_scatter()


def _experts(xs, code, block_e, n_used, w_gate, w_up, w_down, *, n_assign, layer, te):
    d, de = w_gate.shape[2:]
    nch = d // 2 // LANES
    assert nch % 8 == 0, "a token must cover whole (8, 128) tiles"
    n_blocks = block_e.shape[0]
    assert code.shape[0] == n_blocks * te
    by_expert = lambda i, code, be, nu: (layer, be[i], 0, 0)
    grid_spec = pltpu.PrefetchScalarGridSpec(
        num_scalar_prefetch=3,
        grid=(n_blocks,),
        in_specs=[pl.BlockSpec((te * nch, LANES),
                               lambda i, code, be, nu: (jnp.minimum(i, nu[0] - 1), 0)),
                  pl.BlockSpec((None, None, d, de), by_expert),
                  pl.BlockSpec((None, None, d, de), by_expert),
                  pl.BlockSpec((None, None, de, d), by_expert)],
        out_specs=pl.BlockSpec(memory_space=pl.ANY),
        scratch_shapes=[pltpu.VMEM((te * nch, LANES), jnp.uint32),
                        pltpu.VMEM((d, de), BF16), pltpu.VMEM((d, de), BF16),
                        pltpu.VMEM((de, d), BF16),
                        pltpu.SemaphoreType.DMA(())],
    )
    return pl.pallas_call(
        _experts_body,
        out_shape=jax.ShapeDtypeStruct(((n_assign + te) * nch, LANES), jnp.uint32),
        grid_spec=grid_spec,
        compiler_params=_params("arbitrary"),
        name="moe_experts",
    )(code, block_e, n_used, xs, w_gate, w_up, w_down)


def _dispatch(ri, counts, *, te):
    n = ri.shape[1]
    counts = counts[0, :N_EXPERTS]
    padded = (counts + te - 1) // te * te
    pend = jnp.cumsum(padded)
    pstart = pend - padded
    n_blocks = TOP_K * n // te + N_EXPERTS
    cap = n_blocks * te
    expert, rank = ri[:TOP_K], ri[TOP_K:2 * TOP_K]
    first = jnp.sum(jnp.where(expert[None] == jnp.arange(N_EXPERTS)[:, None, None],
                              pstart[:, None, None], 0), axis=0)
    slot = (first + rank).reshape(-1).astype(jnp.int32)
    block_start = jnp.arange(n_blocks, dtype=jnp.int32) * te
    block_e = jnp.minimum(jnp.sum(pend[None, :] <= block_start[:, None], axis=1),
                          N_EXPERTS - 1).astype(jnp.int32)
    n_used = (pend[-1:] // te).astype(jnp.int32)
    pad_start = jnp.concatenate([pstart + counts, pend[-1:]]).astype(jnp.int32)
    pad_len = jnp.concatenate([padded - counts, cap - pend[-1:]]).astype(jnp.int32)
    pad_code = TOP_K * n + jnp.arange(cap, dtype=jnp.int32) % te
    return slot, block_e, n_used, pad_start, pad_len, pad_code


def _combine_body(x_ref, m0_ref, m1_ref, rw_ref, g2_ref, o_ref):
    o_ref[...] = _moe_residual(x_ref[...], m0_ref, m1_ref, rw_ref, g2_ref)


def _combine(x, m, rw, gate2, *, seq, tm):
    n, d = x.shape
    nch = d // 2 // LANES
    per_b = seq // tm
    row_spec = pl.BlockSpec((tm, d), lambda i: (i, 0))
    return pl.pallas_call(
        _combine_body,
        out_shape=jax.ShapeDtypeStruct((n, d), F32),
        grid=(n // tm,),
        in_specs=[row_spec,
                  pl.BlockSpec((tm * nch, LANES), lambda i: (i, 0)),
                  pl.BlockSpec((tm * nch, LANES), lambda i: (n // tm + i, 0)),
                  pl.BlockSpec((tm, rw.shape[1]), lambda i: (i, 0)),
                  pl.BlockSpec((None, 1, d), lambda i: (i // per_b, 0, 0))],
        out_specs=row_spec,
        compiler_params=_params("arbitrary"),
        name="moe_combine",
    )(x, m, m, rw, gate2)


def _tiles(seq):
    tm = min(512, seq)
    tq = min(256, seq)
    te = 256
    return tm, tq, te


def kernel(x, c, w_mod, b_mod, mod_layer, norm1_g, w_in, gm_norm_g, gm_ws, gm_bs, q_norm_g, k_norm_g, out_norm_g, w_out, norm2_g, w_group, b_group, w_route, b_route, w_gate, w_up, w_down):
    batch, seq, d = x.shape
    depth = mod_layer.shape[0]
    n = batch * seq
    tm, tq, te = _tiles(seq)
    assert seq % tm == 0 and seq % tq == 0 and tm % LANES == 0 and tq % LANES == 0
    assert (TOP_K * n) % te == 0 and w_group.shape[2] == N_GROUPS and w_route.shape[2] == N_EXPERTS
    assert w_in.shape[2] * 2 == 5 * w_out.shape[1]

    mod_all = _modulation(c, w_mod, b_mod, mod_layer).reshape(depth, batch, N_MOD, 1, d)
    w_in_b, w_out_b = w_in.astype(BF16), w_out.astype(BF16)
    pad = LANES - N_GROUPS - N_EXPERTS
    w_router = jnp.pad(jnp.concatenate([w_group, w_route], axis=2), ((0, 0), (0, 0), (0, pad)))
    w_router_hi = w_router.astype(BF16)
    w_router_lo = (w_router - w_router_hi.astype(F32)).astype(BF16)
    w_router = jnp.concatenate([w_router_hi, w_router_lo], axis=2)
    b_router = jnp.pad(jnp.concatenate([b_group, b_route], axis=1), ((0, 0), (0, pad)))

    xf = x.reshape(n, d)
    moe = None
    for l in range(depth):
        xf, ya, q, k, v = _mix_in(xf, moe, mod_all[l], norm1_g[l], w_in_b[l], gm_norm_g[l], gm_ws[l],
                                  gm_bs[l], q_norm_g[l], k_norm_g[l], out_norm_g[l], seq=seq, tm=tm)
        yb = _stick_break(q, k, v, out_norm_g[l], batch=batch, seq=seq, tq=tq)
        xf, h2, ri, rw, counts = _mix_out(xf, ya, yb, mod_all[l], w_out_b[l], norm2_g[l],
                                          w_router[l], b_router[l:l + 1], seq=seq, tm=tm)
        slot, block_e, n_used, pad_start, pad_len, pad_code = _dispatch(ri, counts, te=te)
        xs, code = _send_rows(h2, slot, pad_start, pad_len, pad_code, tm=min(4 * tm, seq))
        m = _experts(xs, code, block_e, n_used, w_gate, w_up, w_down,
                     n_assign=slot.shape[0], layer=l, te=te)
        moe = (m, rw, mod_all[l][:, 5])
    out = _combine(xf, *moe, seq=seq, tm=tm)
    return out.reshape(batch, seq, d)
```

```python
import functools

import jax
import jax.numpy as jnp
from jax import lax
from jax.experimental import pallas as pl
from jax.experimental.pallas import tpu as pltpu

F32 = jnp.float32
BF16 = jnp.bfloat16

EPS = 1e-6
LANES = 128
N_MOD = 6
N_GROUPS = 4
EXPERTS_PER_GROUP = 8
N_EXPERTS = N_GROUPS * EXPERTS_PER_GROUP
TOP_K = 2
LOG2E = 1.4426950408889634
LOG2_WEIGHT_FLOOR = -104.0 * LOG2E
Q_FOLD = -(LANES ** -0.5) * LOG2E
ROW_GROUP = 8
VMEM_LIMIT = 56 * 1024 * 1024
HIGH_HALF = 0xFFFF0000


def _rms(x):
    return x * lax.rsqrt(jnp.mean(x * x, axis=-1, keepdims=True) + EPS)


def _gelu(x):
    return 0.5 * x * (1.0 + lax.erf(x * (2.0 ** -0.5)))


def _pack_halves(x):
    half = x.shape[1] // 2
    bits = lax.bitcast_convert_type(x.astype(BF16).astype(F32), jnp.uint32)
    return (lax.shift_right_logical(bits[:, :half], jnp.uint32(16))
            | (bits[:, half:] & jnp.uint32(HIGH_HALF)))


def _unpack_halves(w):
    lo = lax.bitcast_convert_type(lax.shift_left(w, jnp.uint32(16)), F32)
    hi = lax.bitcast_convert_type(w & jnp.uint32(HIGH_HALF), F32)
    return lo, hi


def _store_token_major(ref, base, packed):
    rows, width = packed.shape
    nch = width // LANES
    for c in range(nch):
        ref[pl.ds(base + c, rows, stride=nch), :] = packed[:, c * LANES:(c + 1) * LANES]


def _load_token_major(ref, base, rows, nch):
    return jnp.concatenate([ref[pl.ds(base + c, rows, stride=nch), :] for c in range(nch)], axis=-1)


def _moe_residual(x, m0_ref, m1_ref, rw_ref, g2_ref, r0=0):
    rows, d = x.shape
    nch = d // 2 // LANES
    lo0, hi0 = _unpack_halves(_load_token_major(m0_ref, r0 * nch, rows, nch))
    lo1, hi1 = _unpack_halves(_load_token_major(m1_ref, r0 * nch, rows, nch))
    w1, w2 = rw_ref[r0:r0 + rows, 0:1], rw_ref[r0:r0 + rows, 1:2]
    m = jnp.concatenate([w1 * lo0 + w2 * lo1, w1 * hi0 + w2 * hi1], axis=-1)
    return x + g2_ref[...] * m


def _params(*sem):
    return pltpu.CompilerParams(dimension_semantics=sem, vmem_limit_bytes=VMEM_LIMIT)


def _mod_body(c_ref, w_ref, b_ref, ml_ref, o_ref):
    c = c_ref[...]
    sc = c * jax.nn.sigmoid(c)
    r = jnp.dot(sc, w_ref[...], preferred_element_type=F32,
                precision=lax.Precision.HIGHEST) + b_ref[...]
    for l in range(o_ref.shape[0]):
        o_ref[l] = r + ml_ref[l:l + 1, :]


def _modulation(c, w_mod, b_mod, mod_layer):
    b, d = c.shape
    depth, w = mod_layer.shape
    tn = 1024
    return pl.pallas_call(
        _mod_body,
        out_shape=jax.ShapeDtypeStruct((depth, b, w), F32),
        grid=(w // tn,),
        in_specs=[pl.BlockSpec((b, d), lambda j: (0, 0)),
                  pl.BlockSpec((d, tn), lambda j: (0, j)),
                  pl.BlockSpec((1, tn), lambda j: (0, j)),
                  pl.BlockSpec((depth, tn), lambda j: (0, j))],
        out_specs=pl.BlockSpec((depth, b, tn), lambda j: (0, 0, j)),
        compiler_params=_params("arbitrary"),
        name="modulation",
    )(c, w_mod, b_mod.reshape(1, w), mod_layer)


def _mix_in_body(has_moe, *refs):
    refs = list(refs)
    x_ref = refs.pop(0)
    if has_moe:
        m0_ref, m1_ref, rw_ref, g2_ref = refs.pop(0), refs.pop(0), refs.pop(0), refs.pop(0)
    (mod_ref, g1_ref, win_ref, gmg_ref, ws_ref, bs_ref, qg_ref, kg_ref, og_ref) = refs[:9]
    refs = refs[9:]
    if has_moe:
        xo_ref = refs.pop(0)
    ya_ref, q_ref, k_ref, v_ref = refs

    tm = x_ref.shape[0]
    dq = ya_ref.shape[1]
    shift1, scale1 = mod_ref[0], mod_ref[1]
    gain1 = g1_ref[...] * (1.0 + scale1)
    row = lax.broadcasted_iota(jnp.int32, (LANES, LANES), 0)
    col = lax.broadcasted_iota(jnp.int32, (LANES, LANES), 1)
    causal = row >= col

    def normed(r0, rows):
        x = x_ref[r0:r0 + rows, :]
        if has_moe:
            x = _moe_residual(x, m0_ref, m1_ref, rw_ref, g2_ref, r0)
            xo_ref[r0:r0 + rows, :] = x
        return (_rms(x) * gain1 + shift1).astype(BF16)

    def finish(r0, rows, proj):
        for g in range(dq // LANES):
            sl = slice(g * LANES, (g + 1) * LANES)
            u = _gelu(proj[:, g * LANES:(g + 1) * LANES])
            vg = _gelu(proj[:, dq + g * LANES:dq + (g + 1) * LANES])
            vg = (_rms(vg) * gmg_ref[:, sl]).astype(BF16)
            wg = jnp.where(causal, ws_ref[g], 0.0).astype(BF16)
            for c in range(rows // LANES):
                rs = slice(c * LANES, (c + 1) * LANES)
                s = jnp.dot(wg, vg[rs], preferred_element_type=F32) + bs_ref[g]
                ya = u[rs] * s
                ya_ref[r0 + c * LANES:r0 + (c + 1) * LANES, sl] = (
                    _rms(ya) * og_ref[:, sl]).astype(BF16)
        for hd in range(dq // LANES):
            sl = slice(hd * LANES, (hd + 1) * LANES)
            qh = proj[:, 2 * dq + hd * LANES:2 * dq + (hd + 1) * LANES]
            kh = proj[:, 3 * dq + hd * LANES:3 * dq + (hd + 1) * LANES]
            q_ref[r0:r0 + rows, sl] = (_rms(qh) * (qg_ref[...] * Q_FOLD)).astype(BF16)
            k_ref[r0:r0 + rows, sl] = (_rms(kh) * kg_ref[...]).astype(BF16)
        v_ref[r0:r0 + rows, :] = proj[:, 4 * dq:].astype(BF16)

    n_parts = 2 if tm % (2 * LANES) == 0 else 1
    rows = tm // n_parts
    hs = [normed(p * rows, rows) for p in range(n_parts)]
    projs = [jnp.dot(h, win_ref[...], preferred_element_type=F32) for h in hs]
    for p in range(n_parts):
        finish(p * rows, rows, projs[p])


def _mix_in(x, moe, mod, g1, w_in, gm_g, gm_ws, gm_bs, q_g, k_g, out_g, *, seq, tm):
    n, d = x.shape
    dq = w_in.shape[1] // 5
    ng = dq // LANES
    per_b = seq // tm
    row_spec = pl.BlockSpec((tm, d), lambda i: (i, 0))
    const2 = lambda i: (0, 0)
    in_specs = [row_spec]
    args = [x]
    if moe is not None:
        m, rw, gate2 = moe
        nch = d // 2 // LANES
        in_specs += [pl.BlockSpec((tm * nch, LANES), lambda i: (i, 0)),
                     pl.BlockSpec((tm * nch, LANES), lambda i: (n // tm + i, 0)),
                     pl.BlockSpec((tm, rw.shape[1]), lambda i: (i, 0)),
                     pl.BlockSpec((None, 1, d), lambda i: (i // per_b, 0, 0))]
        args += [m, m, rw, gate2]
    in_specs += [pl.BlockSpec((None, N_MOD, 1, d), lambda i: (i // per_b, 0, 0, 0)),
                 pl.BlockSpec((1, d), const2),
                 pl.BlockSpec(w_in.shape, const2),
                 pl.BlockSpec((1, dq), const2),
                 pl.BlockSpec((ng, LANES, LANES), lambda i: (0, 0, 0)),
                 pl.BlockSpec((ng, LANES, 1), lambda i: (0, 0, 0)),
                 pl.BlockSpec((1, LANES), const2),
                 pl.BlockSpec((1, LANES), const2),
                 pl.BlockSpec((1, dq), const2)]
    args += [mod, g1.reshape(1, d), w_in, gm_g.reshape(1, dq), gm_ws, gm_bs.reshape(ng, LANES, 1),
             q_g.reshape(1, LANES), k_g.reshape(1, LANES), out_g.reshape(1, -1)]
    act = jax.ShapeDtypeStruct((n, dq), BF16)
    act_spec = pl.BlockSpec((tm, dq), lambda i: (i, 0))
    out_shape = [act, act, act, act]
    out_specs = [act_spec, act_spec, act_spec, act_spec]
    if moe is not None:
        out_shape = [jax.ShapeDtypeStruct((n, d), F32)] + out_shape
        out_specs = [row_spec] + out_specs
    outs = pl.pallas_call(
        functools.partial(_mix_in_body, moe is not None),
        out_shape=out_shape, grid=(n // tm,), in_specs=in_specs, out_specs=out_specs,
        compiler_params=_params("arbitrary"), name="mix_in",
    )(*args)
    if moe is None:
        return (x,) + tuple(outs)
    return tuple(outs)


def _stick_break_body(q_ref, k_ref, v_ref, og_ref, o_ref, carry_ref, acc_ref):
    tq, dq = q_ref.shape
    nh = dq // LANES
    nsub = tq // LANES
    qi = pl.program_id(1)
    r2 = lax.broadcasted_iota(jnp.int32, (LANES, 2 * LANES), 0)
    c2 = lax.broadcasted_iota(jnp.int32, (LANES, 2 * LANES), 1)
    tri_ones = jnp.where((r2 > c2) | (c2 >= LANES), 1.0, 0.0).astype(BF16)

    carry_ref[...] = jnp.zeros_like(carry_ref)
    acc_ref[...] = jnp.zeros_like(acc_ref)

    def key_block(r0, kb, masked):
        rows = tq - r0
        heads = [slice(hd * LANES, (hd + 1) * LANES) for hd in range(nh)]
        start = pl.multiple_of(kb * LANES, LANES)
        if masked:
            rr = lax.broadcasted_iota(jnp.int32, (rows, LANES), 0)
            cc = lax.broadcasted_iota(jnp.int32, (rows, LANES), 1)
            keep = cc < rr
        nzs = [lax.dot_general(q_ref[r0:, hs], k_ref[pl.ds(start, LANES), hs],
                               (((1,), (1,)), ((), ())), preferred_element_type=F32)
               for hs in heads]
        lfs, wbs = [], []
        for nz in nzs:
            lf = jnp.minimum(nz, 0.0) - jnp.log2(1.0 + jnp.exp2(jnp.minimum(nz, -nz)))
            if masked:
                lf = jnp.where(keep, lf, 0.0)
            lfs.append(lf)
            wbs.append(jnp.dot(lf.astype(BF16), tri_ones, preferred_element_type=F32))
        for hd, hs in enumerate(heads):
            carry = carry_ref[hd, r0:, :]
            a = jnp.exp2(lfs[hd] - nzs[hd] + wbs[hd][:, :LANES] + carry)
            if masked:
                a = jnp.where(keep, a, 0.0)
            acc_ref[hd, r0:, :] += jnp.dot(a.astype(BF16), v_ref[pl.ds(start, LANES), hs],
                                           preferred_element_type=F32)
            carry_ref[hd, r0:, :] = carry + wbs[hd][:, LANES:]

    for j in reversed(range(nsub)):
        key_block(j * LANES, qi * nsub + j, True)

    def more(state):
        kb, live = state
        return jnp.logical_and(kb >= 0, live)

    def step(state):
        kb, _ = state
        key_block(0, kb, False)
        return kb - 1, jnp.max(carry_ref[...]) > LOG2_WEIGHT_FLOOR

    lax.while_loop(more, step, (qi * nsub - 1, jnp.max(carry_ref[...]) > LOG2_WEIGHT_FLOOR))
    for hd in range(nh):
        hs = slice(hd * LANES, (hd + 1) * LANES)
        o_ref[:, hs] = (_rms(acc_ref[hd]) * og_ref[:, hs]).astype(BF16)


def _stick_break(q, k, v, out_g, *, batch, seq, tq):
    n, dq = q.shape
    nh = dq // LANES
    per_b = seq // tq
    q_spec = pl.BlockSpec((tq, dq), lambda b, i: (b * per_b + i, 0))
    kv_spec = pl.BlockSpec((seq, dq), lambda b, i: (b, 0), pipeline_mode=pl.Buffered(1))
    return pl.pallas_call(
        _stick_break_body,
        out_shape=jax.ShapeDtypeStruct((n, dq), BF16),
        grid=(batch, per_b),
        in_specs=[q_spec, kv_spec, kv_spec, pl.BlockSpec((1, dq), lambda b, i: (0, 1))],
        out_specs=q_spec,
        scratch_shapes=[pltpu.VMEM((nh, tq, LANES), F32), pltpu.VMEM((nh, tq, LANES), F32)],
        compiler_params=_params("arbitrary", "arbitrary"),
        name="stick_break",
    )(q, k, v, out_g.reshape(1, -1))


def _mix_out_body(x_ref, ya_ref, yb_ref, mod_ref, woa_ref, wob_ref, g2_ref, wr_ref, br_ref,
                  x1_ref, h2_ref, ri_ref, rw_ref, cnt_ref, count_scr):
    tm = x_ref.shape[0]

    @pl.when(pl.program_id(0) == 0)
    def _():
        count_scr[...] = jnp.zeros_like(count_scr)

    gate1, shift2, scale2 = mod_ref[2], mod_ref[3], mod_ref[4]
    y = (jnp.dot(ya_ref[...], woa_ref[...], preferred_element_type=F32)
         + jnp.dot(yb_ref[...], wob_ref[...], preferred_element_type=F32))
    x1 = x_ref[...] + gate1 * y
    x1_ref[...] = x1
    h2 = _rms(x1) * g2_ref[...] * (1.0 + scale2) + shift2
    _store_token_major(h2_ref, 0, _pack_halves(h2))

    h_hi = h2.astype(BF16)
    h_lo = (h2 - h_hi.astype(F32)).astype(BF16)
    p = jnp.dot(h_hi, wr_ref[...], preferred_element_type=F32)
    lg = (p[:, :LANES] + p[:, LANES:]
          + jnp.dot(h_lo, wr_ref[:, :LANES], preferred_element_type=F32) + br_ref[...])
    lane = lax.broadcasted_iota(jnp.int32, (tm, LANES), 1)
    lane_f = lane.astype(F32)
    neg = -jnp.inf

    def first_max(vals):
        m = jnp.max(vals, axis=-1, keepdims=True)
        idx = jnp.min(jnp.where(vals == m, lane_f, float(LANES)), axis=-1, keepdims=True)
        return m, idx.astype(jnp.int32)

    is_group = lane < N_GROUPS
    gmax, gidx = first_max(jnp.where(is_group, lg, neg))
    p_group = 1.0 / jnp.sum(jnp.where(is_group, jnp.exp(lg - gmax), 0.0), axis=-1, keepdims=True)
    lo = N_GROUPS + gidx * EXPERTS_PER_GROUP
    el = jnp.where((lane >= lo) & (lane < lo + EXPERTS_PER_GROUP), lg, neg)
    m1, i1 = first_max(el)
    m2, i2 = first_max(jnp.where(lane == i1, neg, el))
    t = jnp.exp(m2 - m1)
    w1 = p_group / (1.0 + t)
    w2 = w1 * t
    e1 = i1 - N_GROUPS
    e2 = i2 - N_GROUPS

    hit1 = lane == e1
    hit2 = lane == e2
    onehot = jnp.where(hit1 | hit2, 1.0, 0.0)
    rr = lax.broadcasted_iota(jnp.int32, (tm, tm), 0)
    cc = lax.broadcasted_iota(jnp.int32, (tm, tm), 1)
    earlier = jnp.where(rr > cc, 1.0, 0.0).astype(BF16)
    base = jnp.dot(earlier, onehot.astype(BF16), preferred_element_type=F32) + count_scr[...]
    r1 = jnp.sum(jnp.where(hit1, base, 0.0), axis=-1, keepdims=True).astype(jnp.int32)
    r2 = jnp.sum(jnp.where(hit2, base, 0.0), axis=-1, keepdims=True).astype(jnp.int32)
    count_scr[...] += jnp.sum(onehot, axis=0, keepdims=True)
    cnt_ref[...] = count_scr[...].astype(jnp.int32)

    fields = jnp.where(lane == 0, e1, jnp.where(lane == 1, e2,
                                                jnp.where(lane == 2, r1,
                                                          jnp.where(lane == 3, r2, 0))))
    ri_ref[...] = fields.T[:ri_ref.shape[0], :]
    l8 = lax.broadcasted_iota(jnp.int32, rw_ref.shape, 1)
    rw_ref[...] = jnp.where(l8 == 0, w1, w2)


def _mix_out(x, ya, yb, mod, w_out, g2, w_router, b_router, *, seq, tm):
    n, d = x.shape
    dq = ya.shape[1]
    nch = d // 2 // LANES
    per_b = seq // tm
    row_spec = pl.BlockSpec((tm, d), lambda i: (i, 0))
    act_spec = pl.BlockSpec((tm, dq), lambda i: (i, 0))
    small_spec = pl.BlockSpec((tm, 8), lambda i: (i, 0))
    const2 = lambda i: (0, 0)
    return pl.pallas_call(
        _mix_out_body,
        out_shape=[jax.ShapeDtypeStruct((n, d), F32),
                   jax.ShapeDtypeStruct((n * nch, LANES), jnp.uint32),
                   jax.ShapeDtypeStruct((8, n), jnp.int32), jax.ShapeDtypeStruct((n, 8), F32),
                   jax.ShapeDtypeStruct((1, LANES), jnp.int32)],
        grid=(n // tm,),
        in_specs=[row_spec, act_spec, act_spec,
                  pl.BlockSpec((None, N_MOD, 1, d), lambda i: (i // per_b, 0, 0, 0)),
                  pl.BlockSpec((dq, d), const2),
                  pl.BlockSpec((dq, d), lambda i: (1, 0)),
                  pl.BlockSpec((1, d), const2),
                  pl.BlockSpec((d, 2 * LANES), const2),
                  pl.BlockSpec((1, LANES), const2)],
        out_specs=[row_spec, pl.BlockSpec((tm * nch, LANES), lambda i: (i, 0)),
                   pl.BlockSpec((8, tm), lambda i: (0, i)), small_spec,
                   pl.BlockSpec((1, LANES), const2)],
        scratch_shapes=[pltpu.VMEM((1, LANES), F32)],
        compiler_params=_params("arbitrary"),
        name="mix_out",
    )(x, ya, yb, mod, w_out, w_out, g2.reshape(1, d), w_router, b_router)


def _send_rows_body(slot_ref, ps_ref, pn_ref, h_ref, pad_code_hbm, xs_hbm, code_hbm,
                    code_ref, zbuf, sem, zsem, csem):
    nch = zbuf.shape[0]
    tm = h_ref.shape[0] // nch
    i = pl.program_id(0)
    n_tok = slot_ref.shape[0] // TOP_K
    n_pad = xs_hbm.shape[0] // nch - slot_ref.shape[0]

    def rows(ref, index):
        return ref.at[pl.ds(pl.multiple_of(index * nch, nch), nch), :]

    @pl.when(i == 0)
    def _():
        fill_code = pltpu.make_async_copy(pad_code_hbm, code_ref, csem)
        fill_code.start()
        fill_code.wait()
        zbuf[...] = jnp.zeros_like(zbuf)

        def region(e, carry):
            first = ps_ref[e]

            def fill(j, c):
                pltpu.make_async_copy(zbuf, rows(xs_hbm, first + j), zsem).start()
                return c

            return lax.fori_loop(0, pn_ref[e], fill, carry)

        lax.fori_loop(0, ps_ref.shape[0], region, 0)
        pltpu.make_async_copy(xs_hbm.at[pl.ds(0, n_pad * nch), :],
                              xs_hbm.at[pl.ds(0, n_pad * nch), :], zsem).wait()

    def group(j, carry):
        for u in range(ROW_GROUP):
            r = j * ROW_GROUP + u
            for k in range(TOP_K):
                a = k * n_tok + i * tm + r
                dst = slot_ref[a]
                code_ref[dst] = a
                pltpu.make_async_copy(rows(h_ref, r), rows(xs_hbm, dst), sem).start(priority=k)
        return carry

    lax.fori_loop(0, tm // ROW_GROUP, group, 0)
    pltpu.make_async_copy(xs_hbm.at[pl.ds(0, TOP_K * tm * nch), :],
                          xs_hbm.at[pl.ds(0, TOP_K * tm * nch), :], sem).wait()

    @pl.when(i == pl.num_programs(0) - 1)
    def _():
        emit_code = pltpu.make_async_copy(code_ref, code_hbm, csem)
        emit_code.start()
        emit_code.wait()


def _send_rows(h2, slot, pad_start, pad_len, pad_code, *, tm):
    cap = pad_code.shape[0]
    nch = h2.shape[0] * TOP_K // slot.shape[0]
    n = h2.shape[0] // nch
    assert tm % ROW_GROUP == 0 and cap - slot.shape[0] > 0
    return pl.pallas_call(
        _send_rows_body,
        out_shape=[jax.ShapeDtypeStruct((cap * nch, LANES), jnp.uint32),
                   jax.ShapeDtypeStruct((cap,), jnp.int32)],
        grid_spec=pltpu.PrefetchScalarGridSpec(
            num_scalar_prefetch=3, grid=(n // tm,),
            in_specs=[pl.BlockSpec((tm * nch, LANES), lambda i, *_: (i, 0)),
                      pl.BlockSpec(memory_space=pl.ANY)],
            out_specs=[pl.BlockSpec(memory_space=pl.ANY), pl.BlockSpec(memory_space=pl.ANY)],
            scratch_shapes=[pltpu.SMEM((cap,), jnp.int32),
                            pltpu.VMEM((nch, LANES), jnp.uint32),
                            pltpu.SemaphoreType.DMA(()), pltpu.SemaphoreType.DMA(()),
                            pltpu.SemaphoreType.DMA(())]),
        compiler_params=_params("arbitrary"),
        name="moe_send_rows",
    )(slot, pad_start, pad_len, h2, pad_code)


def _experts_body(code_ref, be_ref, nu_ref, x_ref, wg_ref, wu_ref, wd_ref,
                  out_hbm, ybuf, wg_b, wu_b, wd_b, ssem):
    nch = wg_b.shape[0] // 2 // LANES
    block_rows = ybuf.shape[0] // 2
    te = block_rows // nch
    spare = out_hbm.shape[0] - 2 * block_rows
    i = pl.program_id(0)
    n_used = nu_ref[0]

    def wait_scatter(buf):
        pltpu.make_async_copy(ybuf.at[pl.ds(0, block_rows), :],
                              out_hbm.at[pl.ds(0, block_rows), :], ssem.at[buf]).wait()

    @pl.when(i == 0)
    def _():
        ybuf[...] = jnp.zeros_like(ybuf)
        clear = pltpu.make_async_copy(ybuf, out_hbm.at[pl.ds(spare, 2 * block_rows), :], ssem.at[0])
        clear.start()
        clear.wait()

    @pl.when(i < n_used)
    def _():
        base = i * te
        buf = i % 2

        @pl.when(jnp.logical_or(i == 0, be_ref[i] != be_ref[jnp.maximum(i - 1, 0)]))
        def _():
            wg_b[...] = wg_ref[...].astype(BF16)
            wu_b[...] = wu_ref[...].astype(BF16)
            wd_b[...] = wd_ref[...].astype(BF16)

        half = nch * LANES
        x_lo, x_hi = _unpack_halves(_load_token_major(x_ref, 0, te, nch))
        x_lo, x_hi = x_lo.astype(BF16), x_hi.astype(BF16)
        g = (jnp.dot(x_lo, wg_b[:half], preferred_element_type=F32)
             + jnp.dot(x_hi, wg_b[half:], preferred_element_type=F32))
        u = (jnp.dot(x_lo, wu_b[:half], preferred_element_type=F32)
             + jnp.dot(x_hi, wu_b[half:], preferred_element_type=F32))
        act = (g * jax.nn.sigmoid(g) * u).astype(BF16)

        @pl.when(i > 1)
        def _():
            wait_scatter(buf)

        ybase = pl.multiple_of(buf * block_rows, block_rows)
        _store_token_major(ybuf, ybase, _pack_halves(jnp.dot(act, wd_b[...],
                                                              preferred_element_type=F32)))
        for r in range(te):
            src = pl.multiple_of(ybase + r * nch, nch)
            dst = pl.multiple_of(code_ref[base + r] * nch, nch)
            pltpu.make_async_copy(ybuf.at[pl.ds(src, nch), :], out_hbm.at[pl.ds(dst, nch), :],
                                  ssem.at[buf]).start(priority=r % 2)

        @pl.when(i == n_used - 1)
        def _():
            wait_scatter(buf)

            @pl.when(i > 0)
            def _():
                wait_scatter(1 - buf)


def _experts(xs, code, block_e, n_used, w_gate, w_up, w_down, *, n_assign, layer, te):
    d, de = w_gate.shape[2:]
    nch = d // 2 // LANES
    assert nch % 8 == 0, "a token must cover whole (8, 128) tiles"
    n_blocks = block_e.shape[0]
    assert code.shape[0] == n_blocks * te
    by_expert = lambda i, code, be, nu: (layer, be[i], 0, 0)
    grid_spec = pltpu.PrefetchScalarGridSpec(
        num_scalar_prefetch=3,
        grid=(n_blocks,),
        in_specs=[pl.BlockSpec((te * nch, LANES),
                               lambda i, code, be, nu: (jnp.minimum(i, nu[0] - 1), 0)),
                  pl.BlockSpec((None, None, d, de), by_expert),
                  pl.BlockSpec((None, None, d, de), by_expert),
                  pl.BlockSpec((None, None, de, d), by_expert)],
        out_specs=pl.BlockSpec(memory_space=pl.ANY),
        scratch_shapes=[pltpu.VMEM((2 * te * nch, LANES), jnp.uint32),
                        pltpu.VMEM((d, de), BF16), pltpu.VMEM((d, de), BF16),
                        pltpu.VMEM((de, d), BF16),
                        pltpu.SemaphoreType.DMA((2,))],
    )
    return pl.pallas_call(
        _experts_body,
        out_shape=jax.ShapeDtypeStruct(((n_assign + 2 * te) * nch, LANES), jnp.uint32),
        grid_spec=grid_spec,
        compiler_params=_params("arbitrary"),
        name="moe_experts",
    )(code, block_e, n_used, xs, w_gate, w_up, w_down)


def _dispatch(ri, counts, *, te):
    n = ri.shape[1]
    counts = counts[0, :N_EXPERTS]
    padded = (counts + te - 1) // te * te
    pend = jnp.cumsum(padded)
    pstart = pend - padded
    n_blocks = TOP_K * n // te + N_EXPERTS
    cap = n_blocks * te
    expert, rank = ri[:TOP_K], ri[TOP_K:2 * TOP_K]
    first = jnp.sum(jnp.where(expert[None] == jnp.arange(N_EXPERTS)[:, None, None],
                              pstart[:, None, None], 0), axis=0)
    slot = (first + rank).reshape(-1).astype(jnp.int32)
    block_start = jnp.arange(n_blocks, dtype=jnp.int32) * te
    block_e = jnp.minimum(jnp.sum(pend[None, :] <= block_start[:, None], axis=1),
                          N_EXPERTS - 1).astype(jnp.int32)
    n_used = (pend[-1:] // te).astype(jnp.int32)
    pad_start = jnp.concatenate([pstart + counts, pend[-1:]]).astype(jnp.int32)
    pad_len = jnp.concatenate([padded - counts, cap - pend[-1:]]).astype(jnp.int32)
    pad_code = TOP_K * n + jnp.arange(cap, dtype=jnp.int32) % (2 * te)
    return slot, block_e, n_used, pad_start, pad_len, pad_code


def _combine_body(x_ref, m0_ref, m1_ref, rw_ref, g2_ref, o_ref):
    o_ref[...] = _moe_residual(x_ref[...], m0_ref, m1_ref, rw_ref, g2_ref)


def _combine(x, m, rw, gate2, *, seq, tm):
    n, d = x.shape
    nch = d // 2 // LANES
    per_b = seq // tm
    row_spec = pl.BlockSpec((tm, d), lambda i: (i, 0))
    return pl.pallas_call(
        _combine_body,
        out_shape=jax.ShapeDtypeStruct((n, d), F32),
        grid=(n // tm,),
        in_specs=[row_spec,
                  pl.BlockSpec((tm * nch, LANES), lambda i: (i, 0)),
                  pl.BlockSpec((tm * nch, LANES), lambda i: (n // tm + i, 0)),
                  pl.BlockSpec((tm, rw.shape[1]), lambda i: (i, 0)),
                  pl.BlockSpec((None, 1, d), lambda i: (i // per_b, 0, 0))],
        out_specs=row_spec,
        compiler_params=_params("arbitrary"),
        name="moe_combine",
    )(x, m, m, rw, gate2)


def _tiles(seq):
    tm = min(512, seq)
    tq = min(256, seq)
    te = 256
    return tm, tq, te


def kernel(x, c, w_mod, b_mod, mod_layer, norm1_g, w_in, gm_norm_g, gm_ws, gm_bs, q_norm_g, k_norm_g, out_norm_g, w_out, norm2_g, w_group, b_group, w_route, b_route, w_gate, w_up, w_down):
    batch, seq, d = x.shape
    depth = mod_layer.shape[0]
    n = batch * seq
    tm, tq, te = _tiles(seq)
    assert seq % tm == 0 and seq % tq == 0 and tm % LANES == 0 and tq % LANES == 0
    assert (TOP_K * n) % te == 0 and w_group.shape[2] == N_GROUPS and w_route.shape[2] == N_EXPERTS
    assert w_in.shape[2] * 2 == 5 * w_out.shape[1]

    mod_all = _modulation(c, w_mod, b_mod, mod_layer).reshape(depth, batch, N_MOD, 1, d)
    w_in_b, w_out_b = w_in.astype(BF16), w_out.astype(BF16)
    pad = LANES - N_GROUPS - N_EXPERTS
    w_router = jnp.pad(jnp.concatenate([w_group, w_route], axis=2), ((0, 0), (0, 0), (0, pad)))
    w_router_hi = w_router.astype(BF16)
    w_router_lo = (w_router - w_router_hi.astype(F32)).astype(BF16)
    w_router = jnp.concatenate([w_router_hi, w_router_lo], axis=2)
    b_router = jnp.pad(jnp.concatenate([b_group, b_route], axis=1), ((0, 0), (0, pad)))

    xf = x.reshape(n, d)
    moe = None
    for l in range(depth):
        xf, ya, q, k, v = _mix_in(xf, moe, mod_all[l], norm1_g[l], w_in_b[l], gm_norm_g[l], gm_ws[l],
                                  gm_bs[l], q_norm_g[l], k_norm_g[l], out_norm_g[l], seq=seq, tm=tm)
        yb = _stick_break(q, k, v, out_norm_g[l], batch=batch, seq=seq, tq=tq)
        xf, h2, ri, rw, counts = _mix_out(xf, ya, yb, mod_all[l], w_out_b[l], norm2_g[l],
                                          w_router[l], b_router[l:l + 1], seq=seq, tm=tm)
        slot, block_e, n_used, pad_start, pad_len, pad_code = _dispatch(ri, counts, te=te)
        xs, code = _send_rows(h2, slot, pad_start, pad_len, pad_code, tm=min(4 * tm, seq))
        m = _experts(xs, code, block_e, n_used, w_gate, w_up, w_down,
                     n_assign=slot.shape[0], layer=l, te=te)
        moe = (m, rw, mod_all[l][:, 5])
    out = _combine(xf, *moe, seq=seq, tm=tm)
    return out.reshape(batch, seq, d)
```

```python
import functools

import jax
import jax.numpy as jnp
from jax import lax
from jax.experimental import pallas as pl
from jax.experimental.pallas import tpu as pltpu

F32 = jnp.float32
BF16 = jnp.bfloat16

EPS = 1e-6
LANES = 128
N_MOD = 6
N_GROUPS = 4
EXPERTS_PER_GROUP = 8
N_EXPERTS = N_GROUPS * EXPERTS_PER_GROUP
TOP_K = 2
LOG2E = 1.4426950408889634
LOG2_WEIGHT_FLOOR = -104.0 * LOG2E
Q_FOLD = -(LANES ** -0.5) * LOG2E
ROW_GROUP = 8
VMEM_LIMIT = 56 * 1024 * 1024
HIGH_HALF = 0xFFFF0000


def _rms(x):
    return x * lax.rsqrt(jnp.mean(x * x, axis=-1, keepdims=True) + EPS)


def _gelu(x):
    return 0.5 * x * (1.0 + lax.erf(x * (2.0 ** -0.5)))


def _pack_halves(x):
    half = x.shape[1] // 2
    bits = lax.bitcast_convert_type(x.astype(BF16).astype(F32), jnp.uint32)
    return (lax.shift_right_logical(bits[:, :half], jnp.uint32(16))
            | (bits[:, half:] & jnp.uint32(HIGH_HALF)))


def _unpack_halves(w):
    lo = lax.bitcast_convert_type(lax.shift_left(w, jnp.uint32(16)), F32)
    hi = lax.bitcast_convert_type(w & jnp.uint32(HIGH_HALF), F32)
    return lo, hi


def _store_token_major(ref, base, packed):
    rows, width = packed.shape
    nch = width // LANES
    for c in range(nch):
        ref[pl.ds(base + c, rows, stride=nch), :] = packed[:, c * LANES:(c + 1) * LANES]


def _load_token_major(ref, base, rows, nch):
    return jnp.concatenate([ref[pl.ds(base + c, rows, stride=nch), :] for c in range(nch)], axis=-1)


def _moe_residual(x, m0_ref, m1_ref, rw_ref, g2_ref, r0=0):
    rows, d = x.shape
    nch = d // 2 // LANES
    lo0, hi0 = _unpack_halves(_load_token_major(m0_ref, r0 * nch, rows, nch))
    lo1, hi1 = _unpack_halves(_load_token_major(m1_ref, r0 * nch, rows, nch))
    w1, w2 = rw_ref[r0:r0 + rows, 0:1], rw_ref[r0:r0 + rows, 1:2]
    m = jnp.concatenate([w1 * lo0 + w2 * lo1, w1 * hi0 + w2 * hi1], axis=-1)
    return x + g2_ref[...] * m


def _params(*sem):
    return pltpu.CompilerParams(dimension_semantics=sem, vmem_limit_bytes=VMEM_LIMIT)


def _mod_body(c_ref, w_ref, b_ref, ml_ref, o_ref):
    c = c_ref[...]
    sc = c * jax.nn.sigmoid(c)
    r = jnp.dot(sc, w_ref[...], preferred_element_type=F32,
                precision=lax.Precision.HIGHEST) + b_ref[...]
    for l in range(o_ref.shape[0]):
        o_ref[l] = r + ml_ref[l:l + 1, :]


def _modulation(c, w_mod, b_mod, mod_layer):
    b, d = c.shape
    depth, w = mod_layer.shape
    tn = 1024
    return pl.pallas_call(
        _mod_body,
        out_shape=jax.ShapeDtypeStruct((depth, b, w), F32),
        grid=(w // tn,),
        in_specs=[pl.BlockSpec((b, d), lambda j: (0, 0)),
                  pl.BlockSpec((d, tn), lambda j: (0, j)),
                  pl.BlockSpec((1, tn), lambda j: (0, j)),
                  pl.BlockSpec((depth, tn), lambda j: (0, j))],
        out_specs=pl.BlockSpec((depth, b, tn), lambda j: (0, 0, j)),
        compiler_params=_params("arbitrary"),
        name="modulation",
    )(c, w_mod, b_mod.reshape(1, w), mod_layer)


def _mix_in_body(has_moe, *refs):
    refs = list(refs)
    x_ref = refs.pop(0)
    if has_moe:
        m0_ref, m1_ref, rw_ref, g2_ref = refs.pop(0), refs.pop(0), refs.pop(0), refs.pop(0)
    (mod_ref, g1_ref, win_ref, gmg_ref, ws_ref, bs_ref, qg_ref, kg_ref, og_ref) = refs[:9]
    refs = refs[9:]
    if has_moe:
        xo_ref = refs.pop(0)
    ya_ref, q_ref, k_ref, v_ref = refs

    tm = x_ref.shape[0]
    dq = ya_ref.shape[1]
    shift1, scale1 = mod_ref[0], mod_ref[1]
    gain1 = g1_ref[...] * (1.0 + scale1)
    row = lax.broadcasted_iota(jnp.int32, (LANES, LANES), 0)
    col = lax.broadcasted_iota(jnp.int32, (LANES, LANES), 1)
    causal = row >= col

    def normed(r0, rows):
        x = x_ref[r0:r0 + rows, :]
        if has_moe:
            x = _moe_residual(x, m0_ref, m1_ref, rw_ref, g2_ref, r0)
            xo_ref[r0:r0 + rows, :] = x
        return (_rms(x) * gain1 + shift1).astype(BF16)

    def finish(r0, rows, proj):
        for g in range(dq // LANES):
            sl = slice(g * LANES, (g + 1) * LANES)
            u = _gelu(proj[:, g * LANES:(g + 1) * LANES])
            vg = _gelu(proj[:, dq + g * LANES:dq + (g + 1) * LANES])
            vg = (_rms(vg) * gmg_ref[:, sl]).astype(BF16)
            wg = jnp.where(causal, ws_ref[g], 0.0).astype(BF16)
            for c in range(rows // LANES):
                rs = slice(c * LANES, (c + 1) * LANES)
                s = jnp.dot(wg, vg[rs], preferred_element_type=F32) + bs_ref[g]
                ya = u[rs] * s
                ya_ref[r0 + c * LANES:r0 + (c + 1) * LANES, sl] = (
                    _rms(ya) * og_ref[:, sl]).astype(BF16)
        for hd in range(dq // LANES):
            sl = slice(hd * LANES, (hd + 1) * LANES)
            qh = proj[:, 2 * dq + hd * LANES:2 * dq + (hd + 1) * LANES]
            kh = proj[:, 3 * dq + hd * LANES:3 * dq + (hd + 1) * LANES]
            q_ref[r0:r0 + rows, sl] = (_rms(qh) * (qg_ref[...] * Q_FOLD)).astype(BF16)
            k_ref[r0:r0 + rows, sl] = (_rms(kh) * kg_ref[...]).astype(BF16)
        v_ref[r0:r0 + rows, :] = proj[:, 4 * dq:].astype(BF16)

    n_parts = 2 if tm % (2 * LANES) == 0 else 1
    rows = tm // n_parts
    hs = [normed(p * rows, rows) for p in range(n_parts)]
    projs = [jnp.dot(h, win_ref[...], preferred_element_type=F32) for h in hs]
    for p in range(n_parts):
        finish(p * rows, rows, projs[p])


def _mix_in(x, moe, mod, g1, w_in, gm_g, gm_ws, gm_bs, q_g, k_g, out_g, *, seq, tm):
    n, d = x.shape
    dq = w_in.shape[1] // 5
    ng = dq // LANES
    per_b = seq // tm
    row_spec = pl.BlockSpec((tm, d), lambda i: (i, 0))
    const2 = lambda i: (0, 0)
    in_specs = [row_spec]
    args = [x]
    if moe is not None:
        m, rw, gate2 = moe
        nch = d // 2 // LANES
        in_specs += [pl.BlockSpec((tm * nch, LANES), lambda i: (i, 0)),
                     pl.BlockSpec((tm * nch, LANES), lambda i: (n // tm + i, 0)),
                     pl.BlockSpec((tm, rw.shape[1]), lambda i: (i, 0)),
                     pl.BlockSpec((None, 1, d), lambda i: (i // per_b, 0, 0))]
        args += [m, m, rw, gate2]
    in_specs += [pl.BlockSpec((None, N_MOD, 1, d), lambda i: (i // per_b, 0, 0, 0)),
                 pl.BlockSpec((1, d), const2),
                 pl.BlockSpec(w_in.shape, const2),
                 pl.BlockSpec((1, dq), const2),
                 pl.BlockSpec((ng, LANES, LANES), lambda i: (0, 0, 0)),
                 pl.BlockSpec((ng, LANES, 1), lambda i: (0, 0, 0)),
                 pl.BlockSpec((1, LANES), const2),
                 pl.BlockSpec((1, LANES), const2),
                 pl.BlockSpec((1, dq), const2)]
    args += [mod, g1.reshape(1, d), w_in, gm_g.reshape(1, dq), gm_ws, gm_bs.reshape(ng, LANES, 1),
             q_g.reshape(1, LANES), k_g.reshape(1, LANES), out_g.reshape(1, -1)]
    act = jax.ShapeDtypeStruct((n, dq), BF16)
    act_spec = pl.BlockSpec((tm, dq), lambda i: (i, 0))
    out_shape = [act, act, act, act]
    out_specs = [act_spec, act_spec, act_spec, act_spec]
    if moe is not None:
        out_shape = [jax.ShapeDtypeStruct((n, d), F32)] + out_shape
        out_specs = [row_spec] + out_specs
    outs = pl.pallas_call(
        functools.partial(_mix_in_body, moe is not None),
        out_shape=out_shape, grid=(n // tm,), in_specs=in_specs, out_specs=out_specs,
        compiler_params=_params("arbitrary"), name="mix_in",
    )(*args)
    if moe is None:
        return (x,) + tuple(outs)
    return tuple(outs)


def _stick_break_body(q_ref, k_ref, v_ref, og_ref, o_ref, carry_ref, acc_ref):
    tq, dq = q_ref.shape
    nh = dq // LANES
    nsub = tq // LANES
    qi = pl.program_id(1)
    r2 = lax.broadcasted_iota(jnp.int32, (LANES, 2 * LANES), 0)
    c2 = lax.broadcasted_iota(jnp.int32, (LANES, 2 * LANES), 1)
    tri_ones = jnp.where((r2 > c2) | (c2 >= LANES), 1.0, 0.0).astype(BF16)

    carry_ref[...] = jnp.zeros_like(carry_ref)
    acc_ref[...] = jnp.zeros_like(acc_ref)

    def key_block(r0, kb, masked):
        rows = tq - r0
        heads = [slice(hd * LANES, (hd + 1) * LANES) for hd in range(nh)]
        start = pl.multiple_of(kb * LANES, LANES)
        if masked:
            rr = lax.broadcasted_iota(jnp.int32, (rows, LANES), 0)
            cc = lax.broadcasted_iota(jnp.int32, (rows, LANES), 1)
            keep = cc < rr
        nzs = [lax.dot_general(q_ref[r0:, hs], k_ref[pl.ds(start, LANES), hs],
                               (((1,), (1,)), ((), ())), preferred_element_type=F32)
               for hs in heads]
        lfs, wbs = [], []
        for nz in nzs:
            lf = jnp.minimum(nz, 0.0) - jnp.log2(1.0 + jnp.exp2(jnp.minimum(nz, -nz)))
            if masked:
                lf = jnp.where(keep, lf, 0.0)
            lfs.append(lf)
            wbs.append(jnp.dot(lf.astype(BF16), tri_ones, preferred_element_type=F32))
        for hd, hs in enumerate(heads):
            carry = carry_ref[hd, r0:, :]
            a = jnp.exp2(lfs[hd] - nzs[hd] + wbs[hd][:, :LANES] + carry)
            if masked:
                a = jnp.where(keep, a, 0.0)
            acc_ref[hd, r0:, :] += jnp.dot(a.astype(BF16), v_ref[pl.ds(start, LANES), hs],
                                           preferred_element_type=F32)
            carry_ref[hd, r0:, :] = carry + wbs[hd][:, LANES:]

    for j in reversed(range(nsub)):
        key_block(j * LANES, qi * nsub + j, True)

    def more(state):
        kb, live = state
        return jnp.logical_and(kb >= 0, live)

    def step(state):
        kb, _ = state
        key_block(0, kb, False)
        return kb - 1, jnp.max(carry_ref[...]) > LOG2_WEIGHT_FLOOR

    lax.while_loop(more, step, (qi * nsub - 1, jnp.max(carry_ref[...]) > LOG2_WEIGHT_FLOOR))
    for hd in range(nh):
        hs = slice(hd * LANES, (hd + 1) * LANES)
        o_ref[:, hs] = (_rms(acc_ref[hd]) * og_ref[:, hs]).astype(BF16)


def _stick_break(q, k, v, out_g, *, batch, seq, tq):
    n, dq = q.shape
    nh = dq // LANES
    per_b = seq // tq
    q_spec = pl.BlockSpec((tq, dq), lambda b, i: (b * per_b + i, 0))
    kv_spec = pl.BlockSpec((seq, dq), lambda b, i: (b, 0), pipeline_mode=pl.Buffered(1))
    return pl.pallas_call(
        _stick_break_body,
        out_shape=jax.ShapeDtypeStruct((n, dq), BF16),
        grid=(batch, per_b),
        in_specs=[q_spec, kv_spec, kv_spec, pl.BlockSpec((1, dq), lambda b, i: (0, 1))],
        out_specs=q_spec,
        scratch_shapes=[pltpu.VMEM((nh, tq, LANES), F32), pltpu.VMEM((nh, tq, LANES), F32)],
        compiler_params=_params("arbitrary", "arbitrary"),
        name="stick_break",
    )(q, k, v, out_g.reshape(1, -1))


def _mix_out_body(x_ref, ya_ref, yb_ref, mod_ref, woa_ref, wob_ref, g2_ref, wr_ref, br_ref,
                  x1_ref, h2_ref, ri_ref, rw_ref, cnt_ref, count_scr):
    tm = x_ref.shape[0]

    @pl.when(pl.program_id(0) == 0)
    def _():
        count_scr[...] = jnp.zeros_like(count_scr)

    gate1, shift2, scale2 = mod_ref[2], mod_ref[3], mod_ref[4]
    y = (jnp.dot(ya_ref[...], woa_ref[...], preferred_element_type=F32)
         + jnp.dot(yb_ref[...], wob_ref[...], preferred_element_type=F32))
    x1 = x_ref[...] + gate1 * y
    x1_ref[...] = x1
    h2 = _rms(x1) * g2_ref[...] * (1.0 + scale2) + shift2
    _store_token_major(h2_ref, 0, _pack_halves(h2))

    h_hi = h2.astype(BF16)
    h_lo = (h2 - h_hi.astype(F32)).astype(BF16)
    p = jnp.dot(h_hi, wr_ref[...], preferred_element_type=F32)
    lg = (p[:, :LANES] + p[:, LANES:]
          + jnp.dot(h_lo, wr_ref[:, :LANES], preferred_element_type=F32) + br_ref[...])
    lane = lax.broadcasted_iota(jnp.int32, (tm, LANES), 1)
    lane_f = lane.astype(F32)
    neg = -jnp.inf

    def first_max(vals):
        m = jnp.max(vals, axis=-1, keepdims=True)
        idx = jnp.min(jnp.where(vals == m, lane_f, float(LANES)), axis=-1, keepdims=True)
        return m, idx.astype(jnp.int32)

    is_group = lane < N_GROUPS
    gmax, gidx = first_max(jnp.where(is_group, lg, neg))
    p_group = 1.0 / jnp.sum(jnp.where(is_group, jnp.exp(lg - gmax), 0.0), axis=-1, keepdims=True)
    lo = N_GROUPS + gidx * EXPERTS_PER_GROUP
    el = jnp.where((lane >= lo) & (lane < lo + EXPERTS_PER_GROUP), lg, neg)
    m1, i1 = first_max(el)
    m2, i2 = first_max(jnp.where(lane == i1, neg, el))
    t = jnp.exp(m2 - m1)
    w1 = p_group / (1.0 + t)
    w2 = w1 * t
    e1 = i1 - N_GROUPS
    e2 = i2 - N_GROUPS

    hit1 = lane == e1
    hit2 = lane == e2
    onehot = jnp.where(hit1 | hit2, 1.0, 0.0)
    rr = lax.broadcasted_iota(jnp.int32, (tm, tm), 0)
    cc = lax.broadcasted_iota(jnp.int32, (tm, tm), 1)
    earlier = jnp.where(rr > cc, 1.0, 0.0).astype(BF16)
    base = jnp.dot(earlier, onehot.astype(BF16), preferred_element_type=F32) + count_scr[...]
    r1 = jnp.sum(jnp.where(hit1, base, 0.0), axis=-1, keepdims=True).astype(jnp.int32)
    r2 = jnp.sum(jnp.where(hit2, base, 0.0), axis=-1, keepdims=True).astype(jnp.int32)
    count_scr[...] += jnp.sum(onehot, axis=0, keepdims=True)
    cnt_ref[...] = count_scr[...].astype(jnp.int32)

    fields = jnp.where(lane == 0, e1, jnp.where(lane == 1, e2,
                                                jnp.where(lane == 2, r1,
                                                          jnp.where(lane == 3, r2, 0))))
    ri_ref[...] = fields.T[:ri_ref.shape[0], :]
    l8 = lax.broadcasted_iota(jnp.int32, rw_ref.shape, 1)
    rw_ref[...] = jnp.where(l8 == 0, w1, w2)


def _mix_out(x, ya, yb, mod, w_out, g2, w_router, b_router, *, seq, tm):
    n, d = x.shape
    dq = ya.shape[1]
    nch = d // 2 // LANES
    per_b = seq // tm
    row_spec = pl.BlockSpec((tm, d), lambda i: (i, 0))
    act_spec = pl.BlockSpec((tm, dq), lambda i: (i, 0))
    small_spec = pl.BlockSpec((tm, 8), lambda i: (i, 0))
    const2 = lambda i: (0, 0)
    return pl.pallas_call(
        _mix_out_body,
        out_shape=[jax.ShapeDtypeStruct((n, d), F32),
                   jax.ShapeDtypeStruct((n * nch, LANES), jnp.uint32),
                   jax.ShapeDtypeStruct((8, n), jnp.int32), jax.ShapeDtypeStruct((n, 8), F32),
                   jax.ShapeDtypeStruct((1, LANES), jnp.int32)],
        grid=(n // tm,),
        in_specs=[row_spec, act_spec, act_spec,
                  pl.BlockSpec((None, N_MOD, 1, d), lambda i: (i // per_b, 0, 0, 0)),
                  pl.BlockSpec((dq, d), const2),
                  pl.BlockSpec((dq, d), lambda i: (1, 0)),
                  pl.BlockSpec((1, d), const2),
                  pl.BlockSpec((d, 2 * LANES), const2),
                  pl.BlockSpec((1, LANES), const2)],
        out_specs=[row_spec, pl.BlockSpec((tm * nch, LANES), lambda i: (i, 0)),
                   pl.BlockSpec((8, tm), lambda i: (0, i)), small_spec,
                   pl.BlockSpec((1, LANES), const2)],
        scratch_shapes=[pltpu.VMEM((1, LANES), F32)],
        compiler_params=_params("arbitrary"),
        name="mix_out",
    )(x, ya, yb, mod, w_out, w_out, g2.reshape(1, d), w_router, b_router)


def _send_rows_body(slot_ref, ps_ref, pn_ref, h_ref, pad_code_hbm, xs_hbm, code_hbm,
                    code_ref, zbuf, sem, zsem, csem):
    nch = zbuf.shape[0]
    tm = h_ref.shape[0] // nch
    i = pl.program_id(0)
    n_tok = slot_ref.shape[0] // TOP_K
    n_pad = xs_hbm.shape[0] // nch - slot_ref.shape[0]

    def rows(ref, index):
        return ref.at[pl.ds(pl.multiple_of(index * nch, nch), nch), :]

    @pl.when(i == 0)
    def _():
        fill_code = pltpu.make_async_copy(pad_code_hbm, code_ref, csem)
        fill_code.start()
        fill_code.wait()
        zbuf[...] = jnp.zeros_like(zbuf)

        def region(e, carry):
            first = ps_ref[e]

            def fill(j, c):
                pltpu.make_async_copy(zbuf, rows(xs_hbm, first + j), zsem).start()
                return c

            return lax.fori_loop(0, pn_ref[e], fill, carry)

        lax.fori_loop(0, ps_ref.shape[0], region, 0)
        pltpu.make_async_copy(xs_hbm.at[pl.ds(0, n_pad * nch), :],
                              xs_hbm.at[pl.ds(0, n_pad * nch), :], zsem).wait()

    def group(j, carry):
        for u in range(ROW_GROUP):
            r = j * ROW_GROUP + u
            for k in range(TOP_K):
                a = k * n_tok + i * tm + r
                dst = slot_ref[a]
                code_ref[dst] = a
                pltpu.make_async_copy(rows(h_ref, r), rows(xs_hbm, dst), sem).start(priority=k)
        return carry

    lax.fori_loop(0, tm // ROW_GROUP, group, 0)
    pltpu.make_async_copy(xs_hbm.at[pl.ds(0, TOP_K * tm * nch), :],
                          xs_hbm.at[pl.ds(0, TOP_K * tm * nch), :], sem).wait()

    @pl.when(i == pl.num_programs(0) - 1)
    def _():
        emit_code = pltpu.make_async_copy(code_ref, code_hbm, csem)
        emit_code.start()
        emit_code.wait()


def _send_rows(h2, slot, pad_start, pad_len, pad_code, *, tm):
    cap = pad_code.shape[0]
    nch = h2.shape[0] * TOP_K // slot.shape[0]
    n = h2.shape[0] // nch
    assert tm % ROW_GROUP == 0 and cap - slot.shape[0] > 0
    return pl.pallas_call(
        _send_rows_body,
        out_shape=[jax.ShapeDtypeStruct((cap * nch, LANES), jnp.uint32),
                   jax.ShapeDtypeStruct((cap,), jnp.int32)],
        grid_spec=pltpu.PrefetchScalarGridSpec(
            num_scalar_prefetch=3, grid=(n // tm,),
            in_specs=[pl.BlockSpec((tm * nch, LANES), lambda i, *_: (i, 0)),
                      pl.BlockSpec(memory_space=pl.ANY)],
            out_specs=[pl.BlockSpec(memory_space=pl.ANY), pl.BlockSpec(memory_space=pl.ANY)],
            scratch_shapes=[pltpu.SMEM((cap,), jnp.int32),
                            pltpu.VMEM((nch, LANES), jnp.uint32),
                            pltpu.SemaphoreType.DMA(()), pltpu.SemaphoreType.DMA(()),
                            pltpu.SemaphoreType.DMA(())]),
        compiler_params=_params("arbitrary"),
        name="moe_send_rows",
    )(slot, pad_start, pad_len, h2, pad_code)


def _experts_body(code_ref, be_ref, nu_ref, x_ref, wg_ref, wu_ref, wd_ref,
                  out_hbm, ybuf, wg_b, wu_b, wd_b, ssem):
    nch = wg_b.shape[0] // 2 // LANES
    block_rows = ybuf.shape[0] // 2
    te = block_rows // nch
    spare = out_hbm.shape[0] - 2 * block_rows
    i = pl.program_id(0)
    n_used = nu_ref[0]

    def wait_scatter(buf):
        pltpu.make_async_copy(ybuf.at[pl.ds(0, block_rows), :],
                              out_hbm.at[pl.ds(0, block_rows), :], ssem.at[buf]).wait()

    @pl.when(i == 0)
    def _():
        ybuf[...] = jnp.zeros_like(ybuf)
        clear = pltpu.make_async_copy(ybuf, out_hbm.at[pl.ds(spare, 2 * block_rows), :], ssem.at[0])
        clear.start()
        clear.wait()

    @pl.when(i < n_used)
    def _():
        base = i * te
        buf = i % 2

        @pl.when(jnp.logical_or(i == 0, be_ref[i] != be_ref[jnp.maximum(i - 1, 0)]))
        def _():
            wg_b[...] = wg_ref[...].astype(BF16)
            wu_b[...] = wu_ref[...].astype(BF16)
            wd_b[...] = wd_ref[...].astype(BF16)

        half = nch * LANES
        x_lo, x_hi = _unpack_halves(_load_token_major(x_ref, 0, te, nch))
        x_lo, x_hi = x_lo.astype(BF16), x_hi.astype(BF16)
        g = (jnp.dot(x_lo, wg_b[:half], preferred_element_type=F32)
             + jnp.dot(x_hi, wg_b[half:], preferred_element_type=F32))
        u = (jnp.dot(x_lo, wu_b[:half], preferred_element_type=F32)
             + jnp.dot(x_hi, wu_b[half:], preferred_element_type=F32))
        act = (g * jax.nn.sigmoid(g) * u).astype(BF16)

        @pl.when(i > 1)
        def _():
            wait_scatter(buf)

        ybase = pl.multiple_of(buf * block_rows, block_rows)
        _store_token_major(ybuf, ybase, _pack_halves(jnp.dot(act, wd_b[...],
                                                              preferred_element_type=F32)))
        for r in range(te):
            src = pl.multiple_of(ybase + r * nch, nch)
            dst = pl.multiple_of(code_ref[base + r] * nch, nch)
            pltpu.make_async_copy(ybuf.at[pl.ds(src, nch), :], out_hbm.at[pl.ds(dst, nch), :],
                                  ssem.at[buf]).start(priority=r % 2)

        @pl.when(i == n_used - 1)
        def _():
            wait_scatter(buf)

            @pl.when(i > 0)
            def _():
                wait_scatter(1 - buf)


def _experts(xs, code, block_e, n_used, w_gate, w_up, w_down, *, n_assign, layer, te):
    d, de = w_gate.shape[2:]
    nch = d // 2 // LANES
    assert nch % 8 == 0, "a token must cover whole (8, 128) tiles"
    n_blocks = block_e.shape[0]
    assert code.shape[0] == n_blocks * te
    by_expert = lambda i, code, be, nu: (layer, be[i], 0, 0)
    grid_spec = pltpu.PrefetchScalarGridSpec(
        num_scalar_prefetch=3,
        grid=(n_blocks,),
        in_specs=[pl.BlockSpec((te * nch, LANES),
                               lambda i, code, be, nu: (jnp.minimum(i, nu[0] - 1), 0)),
                  pl.BlockSpec((None, None, d, de), by_expert),
                  pl.BlockSpec((None, None, d, de), by_expert),
                  pl.BlockSpec((None, None, de, d), by_expert)],
        out_specs=pl.BlockSpec(memory_space=pl.ANY),
        scratch_shapes=[pltpu.VMEM((2 * te * nch, LANES), jnp.uint32),
                        pltpu.VMEM((d, de), BF16), pltpu.VMEM((d, de), BF16),
                        pltpu.VMEM((de, d), BF16),
                        pltpu.SemaphoreType.DMA((2,))],
    )
    return pl.pallas_call(
        _experts_body,
        out_shape=jax.ShapeDtypeStruct(((n_assign + 2 * te) * nch, LANES), jnp.uint32),
        grid_spec=grid_spec,
        compiler_params=_params("arbitrary"),
        name="moe_experts",
    )(code, block_e, n_used, xs, w_gate, w_up, w_down)


def _dispatch(ri, counts, *, te):
    n = ri.shape[1]
    counts = counts[0, :N_EXPERTS]
    padded = (counts + te - 1) // te * te
    pend = jnp.cumsum(padded)
    pstart = pend - padded
    n_blocks = TOP_K * n // te + N_EXPERTS
    cap = n_blocks * te
    expert, rank = ri[:TOP_K], ri[TOP_K:2 * TOP_K]
    first = jnp.sum(jnp.where(expert[None] == jnp.arange(N_EXPERTS)[:, None, None],
                              pstart[:, None, None], 0), axis=0)
    slot = (first + rank).reshape(-1).astype(jnp.int32)
    block_start = jnp.arange(n_blocks, dtype=jnp.int32) * te
    block_e = jnp.minimum(jnp.sum(pend[None, :] <= block_start[:, None], axis=1),
                          N_EXPERTS - 1).astype(jnp.int32)
    n_used = (pend[-1:] // te).astype(jnp.int32)
    pad_start = jnp.concatenate([pstart + counts, pend[-1:]]).astype(jnp.int32)
    pad_len = jnp.concatenate([padded - counts, cap - pend[-1:]]).astype(jnp.int32)
    pad_code = TOP_K * n + jnp.arange(cap, dtype=jnp.int32) % (2 * te)
    return slot, block_e, n_used, pad_start, pad_len, pad_code


def _combine_body(x_ref, m0_ref, m1_ref, rw_ref, g2_ref, o_ref):
    o_ref[...] = _moe_residual(x_ref[...], m0_ref, m1_ref, rw_ref, g2_ref)


def _combine(x, m, rw, gate2, *, seq, tm):
    n, d = x.shape
    nch = d // 2 // LANES
    per_b = seq // tm
    row_spec = pl.BlockSpec((tm, d), lambda i: (i, 0))
    return pl.pallas_call(
        _combine_body,
        out_shape=jax.ShapeDtypeStruct((n, d), F32),
        grid=(n // tm,),
        in_specs=[row_spec,
                  pl.BlockSpec((tm * nch, LANES), lambda i: (i, 0)),
                  pl.BlockSpec((tm * nch, LANES), lambda i: (n // tm + i, 0)),
                  pl.BlockSpec((tm, rw.shape[1]), lambda i: (i, 0)),
                  pl.BlockSpec((None, 1, d), lambda i: (i // per_b, 0, 0))],
        out_specs=row_spec,
        compiler_params=_params("arbitrary"),
        name="moe_combine",
    )(x, m, m, rw, gate2)


def _tiles(seq):
    tm = min(512, seq)
    tq = min(256, seq)
    te = 512
    return tm, tq, te


def kernel(x, c, w_mod, b_mod, mod_layer, norm1_g, w_in, gm_norm_g, gm_ws, gm_bs, q_norm_g, k_norm_g, out_norm_g, w_out, norm2_g, w_group, b_group, w_route, b_route, w_gate, w_up, w_down):
    batch, seq, d = x.shape
    depth = mod_layer.shape[0]
    n = batch * seq
    tm, tq, te = _tiles(seq)
    assert seq % tm == 0 and seq % tq == 0 and tm % LANES == 0 and tq % LANES == 0
    assert (TOP_K * n) % te == 0 and w_group.shape[2] == N_GROUPS and w_route.shape[2] == N_EXPERTS
    assert w_in.shape[2] * 2 == 5 * w_out.shape[1]

    mod_all = _modulation(c, w_mod, b_mod, mod_layer).reshape(depth, batch, N_MOD, 1, d)
    w_in_b, w_out_b = w_in.astype(BF16), w_out.astype(BF16)
    pad = LANES - N_GROUPS - N_EXPERTS
    w_router = jnp.pad(jnp.concatenate([w_group, w_route], axis=2), ((0, 0), (0, 0), (0, pad)))
    w_router_hi = w_router.astype(BF16)
    w_router_lo = (w_router - w_router_hi.astype(F32)).astype(BF16)
    w_router = jnp.concatenate([w_router_hi, w_router_lo], axis=2)
    b_router = jnp.pad(jnp.concatenate([b_group, b_route], axis=1), ((0, 0), (0, pad)))

    xf = x.reshape(n, d)
    moe = None
    for l in range(depth):
        xf, ya, q, k, v = _mix_in(xf, moe, mod_all[l], norm1_g[l], w_in_b[l], gm_norm_g[l], gm_ws[l],
                                  gm_bs[l], q_norm_g[l], k_norm_g[l], out_norm_g[l], seq=seq, tm=tm)
        yb = _stick_break(q, k, v, out_norm_g[l], batch=batch, seq=seq, tq=tq)
        xf, h2, ri, rw, counts = _mix_out(xf, ya, yb, mod_all[l], w_out_b[l], norm2_g[l],
                                          w_router[l], b_router[l:l + 1], seq=seq, tm=tm)
        slot, block_e, n_used, pad_start, pad_len, pad_code = _dispatch(ri, counts, te=te)
        xs, code = _send_rows(h2, slot, pad_start, pad_len, pad_code, tm=min(4 * tm, seq))
        m = _experts(xs, code, block_e, n_used, w_gate, w_up, w_down,
                     n_assign=slot.shape[0], layer=l, te=te)
        moe = (m, rw, mod_all[l][:, 5])
    out = _combine(xf, *moe, seq=seq, tm=tm)
    return out.reshape(batch, seq, d)
```

```python
import functools

import jax
import jax.numpy as jnp
from jax import lax
from jax.experimental import pallas as pl
from jax.experimental.pallas import tpu as pltpu

F32 = jnp.float32
BF16 = jnp.bfloat16

EPS = 1e-6
LANES = 128
N_MOD = 6
N_GROUPS = 4
EXPERTS_PER_GROUP = 8
N_EXPERTS = N_GROUPS * EXPERTS_PER_GROUP
TOP_K = 2
LOG2E = 1.4426950408889634
LOG2_WEIGHT_FLOOR = -104.0 * LOG2E
Q_FOLD = -(LANES ** -0.5) * LOG2E
ROW_GROUP = 8
VMEM_LIMIT = 56 * 1024 * 1024
HIGH_HALF = 0xFFFF0000


def _rms(x):
    return x * lax.rsqrt(jnp.mean(x * x, axis=-1, keepdims=True) + EPS)


def _gelu(x):
    return 0.5 * x * (1.0 + lax.erf(x * (2.0 ** -0.5)))


def _pack_halves(x):
    half = x.shape[1] // 2
    bits = lax.bitcast_convert_type(x.astype(BF16).astype(F32), jnp.uint32)
    return (lax.shift_right_logical(bits[:, :half], jnp.uint32(16))
            | (bits[:, half:] & jnp.uint32(HIGH_HALF)))


def _unpack_halves(w):
    lo = lax.bitcast_convert_type(lax.shift_left(w, jnp.uint32(16)), F32)
    hi = lax.bitcast_convert_type(w & jnp.uint32(HIGH_HALF), F32)
    return lo, hi


def _store_token_major(ref, base, packed):
    rows, width = packed.shape
    nch = width // LANES
    for c in range(nch):
        ref[pl.ds(base + c, rows, stride=nch), :] = packed[:, c * LANES:(c + 1) * LANES]


def _load_token_major(ref, base, rows, nch):
    return jnp.concatenate([ref[pl.ds(base + c, rows, stride=nch), :] for c in range(nch)], axis=-1)


def _moe_residual(x, m0_ref, m1_ref, rw_ref, g2_ref, r0=0):
    rows, d = x.shape
    nch = d // 2 // LANES
    lo0, hi0 = _unpack_halves(_load_token_major(m0_ref, r0 * nch, rows, nch))
    lo1, hi1 = _unpack_halves(_load_token_major(m1_ref, r0 * nch, rows, nch))
    w1, w2 = rw_ref[r0:r0 + rows, 0:1], rw_ref[r0:r0 + rows, 1:2]
    m = jnp.concatenate([w1 * lo0 + w2 * lo1, w1 * hi0 + w2 * hi1], axis=-1)
    return x + g2_ref[...] * m


def _params(*sem):
    return pltpu.CompilerParams(dimension_semantics=sem, vmem_limit_bytes=VMEM_LIMIT)


def _mod_body(c_ref, w_ref, b_ref, ml_ref, o_ref):
    c = c_ref[...]
    sc = c * jax.nn.sigmoid(c)
    r = jnp.dot(sc, w_ref[...], preferred_element_type=F32,
                precision=lax.Precision.HIGHEST) + b_ref[...]
    for l in range(o_ref.shape[0]):
        o_ref[l] = r + ml_ref[l:l + 1, :]


def _modulation(c, w_mod, b_mod, mod_layer):
    b, d = c.shape
    depth, w = mod_layer.shape
    tn = 1024
    return pl.pallas_call(
        _mod_body,
        out_shape=jax.ShapeDtypeStruct((depth, b, w), F32),
        grid=(w // tn,),
        in_specs=[pl.BlockSpec((b, d), lambda j: (0, 0)),
                  pl.BlockSpec((d, tn), lambda j: (0, j)),
                  pl.BlockSpec((1, tn), lambda j: (0, j)),
                  pl.BlockSpec((depth, tn), lambda j: (0, j))],
        out_specs=pl.BlockSpec((depth, b, tn), lambda j: (0, 0, j)),
        compiler_params=_params("arbitrary"),
        name="modulation",
    )(c, w_mod, b_mod.reshape(1, w), mod_layer)


def _mix_in_body(has_moe, *refs):
    refs = list(refs)
    x_ref = refs.pop(0)
    if has_moe:
        m0_ref, m1_ref, rw_ref, g2_ref = refs.pop(0), refs.pop(0), refs.pop(0), refs.pop(0)
    (mod_ref, g1_ref, win_ref, gmg_ref, ws_ref, bs_ref, qg_ref, kg_ref, og_ref) = refs[:9]
    refs = refs[9:]
    if has_moe:
        xo_ref = refs.pop(0)
    ya_ref, q_ref, k_ref, v_ref = refs

    tm = x_ref.shape[0]
    dq = ya_ref.shape[1]
    shift1, scale1 = mod_ref[0], mod_ref[1]
    gain1 = g1_ref[...] * (1.0 + scale1)
    row = lax.broadcasted_iota(jnp.int32, (LANES, LANES), 0)
    col = lax.broadcasted_iota(jnp.int32, (LANES, LANES), 1)
    causal = row >= col

    def normed(r0, rows):
        x = x_ref[r0:r0 + rows, :]
        if has_moe:
            x = _moe_residual(x, m0_ref, m1_ref, rw_ref, g2_ref, r0)
            xo_ref[r0:r0 + rows, :] = x
        return (_rms(x) * gain1 + shift1).astype(BF16)

    def finish(r0, rows, proj):
        for g in range(dq // LANES):
            sl = slice(g * LANES, (g + 1) * LANES)
            u = _gelu(proj[:, g * LANES:(g + 1) * LANES])
            vg = _gelu(proj[:, dq + g * LANES:dq + (g + 1) * LANES])
            vg = (_rms(vg) * gmg_ref[:, sl]).astype(BF16)
            wg = jnp.where(causal, ws_ref[g], 0.0).astype(BF16)
            for c in range(rows // LANES):
                rs = slice(c * LANES, (c + 1) * LANES)
                s = jnp.dot(wg, vg[rs], preferred_element_type=F32) + bs_ref[g]
                ya = u[rs] * s
                ya_ref[r0 + c * LANES:r0 + (c + 1) * LANES, sl] = (
                    _rms(ya) * og_ref[:, sl]).astype(BF16)
        for hd in range(dq // LANES):
            sl = slice(hd * LANES, (hd + 1) * LANES)
            qh = proj[:, 2 * dq + hd * LANES:2 * dq + (hd + 1) * LANES]
            kh = proj[:, 3 * dq + hd * LANES:3 * dq + (hd + 1) * LANES]
            q_ref[r0:r0 + rows, sl] = (_rms(qh) * (qg_ref[...] * Q_FOLD)).astype(BF16)
            k_ref[r0:r0 + rows, sl] = (_rms(kh) * kg_ref[...]).astype(BF16)
        v_ref[r0:r0 + rows, :] = proj[:, 4 * dq:].astype(BF16)

    n_parts = 2 if tm % (2 * LANES) == 0 else 1
    rows = tm // n_parts
    hs = [normed(p * rows, rows) for p in range(n_parts)]
    projs = [jnp.dot(h, win_ref[...], preferred_element_type=F32) for h in hs]
    for p in range(n_parts):
        finish(p * rows, rows, projs[p])


def _mix_in(x, moe, mod, g1, w_in, gm_g, gm_ws, gm_bs, q_g, k_g, out_g, *, seq, tm):
    n, d = x.shape
    dq = w_in.shape[1] // 5
    ng = dq // LANES
    per_b = seq // tm
    row_spec = pl.BlockSpec((tm, d), lambda i: (i, 0))
    const2 = lambda i: (0, 0)
    in_specs = [row_spec]
    args = [x]
    if moe is not None:
        m, rw, gate2 = moe
        nch = d // 2 // LANES
        in_specs += [pl.BlockSpec((tm * nch, LANES), lambda i: (i, 0)),
                     pl.BlockSpec((tm * nch, LANES), lambda i: (n // tm + i, 0)),
                     pl.BlockSpec((tm, rw.shape[1]), lambda i: (i, 0)),
                     pl.BlockSpec((None, 1, d), lambda i: (i // per_b, 0, 0))]
        args += [m, m, rw, gate2]
    in_specs += [pl.BlockSpec((None, N_MOD, 1, d), lambda i: (i // per_b, 0, 0, 0)),
                 pl.BlockSpec((1, d), const2),
                 pl.BlockSpec(w_in.shape, const2),
                 pl.BlockSpec((1, dq), const2),
                 pl.BlockSpec((ng, LANES, LANES), lambda i: (0, 0, 0)),
                 pl.BlockSpec((ng, LANES, 1), lambda i: (0, 0, 0)),
                 pl.BlockSpec((1, LANES), const2),
                 pl.BlockSpec((1, LANES), const2),
                 pl.BlockSpec((1, dq), const2)]
    args += [mod, g1.reshape(1, d), w_in, gm_g.reshape(1, dq), gm_ws, gm_bs.reshape(ng, LANES, 1),
             q_g.reshape(1, LANES), k_g.reshape(1, LANES), out_g.reshape(1, -1)]
    act = jax.ShapeDtypeStruct((n, dq), BF16)
    act_spec = pl.BlockSpec((tm, dq), lambda i: (i, 0))
    out_shape = [act, act, act, act]
    out_specs = [act_spec, act_spec, act_spec, act_spec]
    if moe is not None:
        out_shape = [jax.ShapeDtypeStruct((n, d), F32)] + out_shape
        out_specs = [row_spec] + out_specs
    outs = pl.pallas_call(
        functools.partial(_mix_in_body, moe is not None),
        out_shape=out_shape, grid=(n // tm,), in_specs=in_specs, out_specs=out_specs,
        compiler_params=_params("arbitrary"), name="mix_in",
    )(*args)
    if moe is None:
        return (x,) + tuple(outs)
    return tuple(outs)


def _stick_break_body(q_ref, k_ref, v_ref, og_ref, o_ref, carry_ref, acc_ref):
    tq, dq = q_ref.shape
    nh = dq // LANES
    nsub = tq // LANES
    qi = pl.program_id(1)
    r2 = lax.broadcasted_iota(jnp.int32, (LANES, 2 * LANES), 0)
    c2 = lax.broadcasted_iota(jnp.int32, (LANES, 2 * LANES), 1)
    tri_ones = jnp.where((r2 > c2) | (c2 >= LANES), 1.0, 0.0).astype(BF16)

    carry_ref[...] = jnp.zeros_like(carry_ref)
    acc_ref[...] = jnp.zeros_like(acc_ref)

    def key_block(r0, kb, masked):
        rows = tq - r0
        heads = [slice(hd * LANES, (hd + 1) * LANES) for hd in range(nh)]
        start = pl.multiple_of(kb * LANES, LANES)
        if masked:
            rr = lax.broadcasted_iota(jnp.int32, (rows, LANES), 0)
            cc = lax.broadcasted_iota(jnp.int32, (rows, LANES), 1)
            keep = cc < rr
        nzs = [lax.dot_general(q_ref[r0:, hs], k_ref[pl.ds(start, LANES), hs],
                               (((1,), (1,)), ((), ())), preferred_element_type=F32)
               for hs in heads]
        lfs, wbs = [], []
        for nz in nzs:
            lf = jnp.minimum(nz, 0.0) - jnp.log2(1.0 + jnp.exp2(jnp.minimum(nz, -nz)))
            if masked:
                lf = jnp.where(keep, lf, 0.0)
            lfs.append(lf)
            wbs.append(jnp.dot(lf.astype(BF16), tri_ones, preferred_element_type=F32))
        for hd, hs in enumerate(heads):
            carry = carry_ref[hd, r0:, :]
            a = jnp.exp2(lfs[hd] - nzs[hd] + wbs[hd][:, :LANES] + carry)
            if masked:
                a = jnp.where(keep, a, 0.0)
            acc_ref[hd, r0:, :] += jnp.dot(a.astype(BF16), v_ref[pl.ds(start, LANES), hs],
                                           preferred_element_type=F32)
            carry_ref[hd, r0:, :] = carry + wbs[hd][:, LANES:]

    for j in reversed(range(nsub)):
        key_block(j * LANES, qi * nsub + j, True)

    def more(state):
        kb, live = state
        return jnp.logical_and(kb >= 0, live)

    def step(state):
        kb, _ = state
        key_block(0, kb, False)
        return kb - 1, jnp.max(carry_ref[...]) > LOG2_WEIGHT_FLOOR

    lax.while_loop(more, step, (qi * nsub - 1, jnp.max(carry_ref[...]) > LOG2_WEIGHT_FLOOR))
    for hd in range(nh):
        hs = slice(hd * LANES, (hd + 1) * LANES)
        o_ref[:, hs] = (_rms(acc_ref[hd]) * og_ref[:, hs]).astype(BF16)


def _stick_break(q, k, v, out_g, *, batch, seq, tq):
    n, dq = q.shape
    nh = dq // LANES
    per_b = seq // tq
    q_spec = pl.BlockSpec((tq, dq), lambda b, i: (b * per_b + i, 0))
    kv_spec = pl.BlockSpec((seq, dq), lambda b, i: (b, 0), pipeline_mode=pl.Buffered(1))
    return pl.pallas_call(
        _stick_break_body,
        out_shape=jax.ShapeDtypeStruct((n, dq), BF16),
        grid=(batch, per_b),
        in_specs=[q_spec, kv_spec, kv_spec, pl.BlockSpec((1, dq), lambda b, i: (0, 1))],
        out_specs=q_spec,
        scratch_shapes=[pltpu.VMEM((nh, tq, LANES), F32), pltpu.VMEM((nh, tq, LANES), F32)],
        compiler_params=_params("arbitrary", "arbitrary"),
        name="stick_break",
    )(q, k, v, out_g.reshape(1, -1))


def _mix_out_body(x_ref, ya_ref, yb_ref, mod_ref, woa_ref, wob_ref, g2_ref, wr_ref, br_ref,
                  x1_ref, h2_ref, ri_ref, rw_ref, cnt_ref, count_scr):
    tm = x_ref.shape[0]

    @pl.when(pl.program_id(0) == 0)
    def _():
        count_scr[...] = jnp.zeros_like(count_scr)

    gate1, shift2, scale2 = mod_ref[2], mod_ref[3], mod_ref[4]
    y = (jnp.dot(ya_ref[...], woa_ref[...], preferred_element_type=F32)
         + jnp.dot(yb_ref[...], wob_ref[...], preferred_element_type=F32))
    x1 = x_ref[...] + gate1 * y
    x1_ref[...] = x1
    h2 = _rms(x1) * g2_ref[...] * (1.0 + scale2) + shift2
    _store_token_major(h2_ref, 0, _pack_halves(h2))

    h_hi = h2.astype(BF16)
    h_lo = (h2 - h_hi.astype(F32)).astype(BF16)
    p = jnp.dot(h_hi, wr_ref[...], preferred_element_type=F32)
    lg = (p[:, :LANES] + p[:, LANES:]
          + jnp.dot(h_lo, wr_ref[:, :LANES], preferred_element_type=F32) + br_ref[...])
    lane = lax.broadcasted_iota(jnp.int32, (tm, LANES), 1)
    lane_f = lane.astype(F32)
    neg = -jnp.inf

    def first_max(vals):
        m = jnp.max(vals, axis=-1, keepdims=True)
        idx = jnp.min(jnp.where(vals == m, lane_f, float(LANES)), axis=-1, keepdims=True)
        return m, idx.astype(jnp.int32)

    is_group = lane < N_GROUPS
    gmax, gidx = first_max(jnp.where(is_group, lg, neg))
    p_group = 1.0 / jnp.sum(jnp.where(is_group, jnp.exp(lg - gmax), 0.0), axis=-1, keepdims=True)
    lo = N_GROUPS + gidx * EXPERTS_PER_GROUP
    el = jnp.where((lane >= lo) & (lane < lo + EXPERTS_PER_GROUP), lg, neg)
    m1, i1 = first_max(el)
    m2, i2 = first_max(jnp.where(lane == i1, neg, el))
    t = jnp.exp(m2 - m1)
    w1 = p_group / (1.0 + t)
    w2 = w1 * t
    e1 = i1 - N_GROUPS
    e2 = i2 - N_GROUPS

    hit1 = lane == e1
    hit2 = lane == e2
    onehot = jnp.where(hit1 | hit2, 1.0, 0.0)
    rr = lax.broadcasted_iota(jnp.int32, (tm, tm), 0)
    cc = lax.broadcasted_iota(jnp.int32, (tm, tm), 1)
    earlier = jnp.where(rr > cc, 1.0, 0.0).astype(BF16)
    base = jnp.dot(earlier, onehot.astype(BF16), preferred_element_type=F32) + count_scr[...]
    r1 = jnp.sum(jnp.where(hit1, base, 0.0), axis=-1, keepdims=True).astype(jnp.int32)
    r2 = jnp.sum(jnp.where(hit2, base, 0.0), axis=-1, keepdims=True).astype(jnp.int32)
    count_scr[...] += jnp.sum(onehot, axis=0, keepdims=True)
    cnt_ref[...] = count_scr[...].astype(jnp.int32)

    fields = jnp.where(lane == 0, e1, jnp.where(lane == 1, e2,
                                                jnp.where(lane == 2, r1,
                                                          jnp.where(lane == 3, r2, 0))))
    ri_ref[...] = fields.T[:ri_ref.shape[0], :]
    l8 = lax.broadcasted_iota(jnp.int32, rw_ref.shape, 1)
    rw_ref[...] = jnp.where(l8 == 0, w1, w2)


def _mix_out(x, ya, yb, mod, w_out, g2, w_router, b_router, *, seq, tm):
    n, d = x.shape
    dq = ya.shape[1]
    nch = d // 2 // LANES
    per_b = seq // tm
    row_spec = pl.BlockSpec((tm, d), lambda i: (i, 0))
    act_spec = pl.BlockSpec((tm, dq), lambda i: (i, 0))
    small_spec = pl.BlockSpec((tm, 8), lambda i: (i, 0))
    const2 = lambda i: (0, 0)
    return pl.pallas_call(
        _mix_out_body,
        out_shape=[jax.ShapeDtypeStruct((n, d), F32),
                   jax.ShapeDtypeStruct((n * nch, LANES), jnp.uint32),
                   jax.ShapeDtypeStruct((8, n), jnp.int32), jax.ShapeDtypeStruct((n, 8), F32),
                   jax.ShapeDtypeStruct((1, LANES), jnp.int32)],
        grid=(n // tm,),
        in_specs=[row_spec, act_spec, act_spec,
                  pl.BlockSpec((None, N_MOD, 1, d), lambda i: (i // per_b, 0, 0, 0)),
                  pl.BlockSpec((dq, d), const2),
                  pl.BlockSpec((dq, d), lambda i: (1, 0)),
                  pl.BlockSpec((1, d), const2),
                  pl.BlockSpec((d, 2 * LANES), const2),
                  pl.BlockSpec((1, LANES), const2)],
        out_specs=[row_spec, pl.BlockSpec((tm * nch, LANES), lambda i: (i, 0)),
                   pl.BlockSpec((8, tm), lambda i: (0, i)), small_spec,
                   pl.BlockSpec((1, LANES), const2)],
        scratch_shapes=[pltpu.VMEM((1, LANES), F32)],
        compiler_params=_params("arbitrary"),
        name="mix_out",
    )(x, ya, yb, mod, w_out, w_out, g2.reshape(1, d), w_router, b_router)


def _send_rows_body(slot_ref, ps_ref, pn_ref, h_ref, pad_code_hbm, xs_hbm, code_hbm,
                    code_ref, zbuf, sem, zsem, csem):
    nch = zbuf.shape[0] // ROW_GROUP
    tm = h_ref.shape[0] // nch
    i = pl.program_id(0)
    n_tok = slot_ref.shape[0] // TOP_K
    n_pad = xs_hbm.shape[0] // nch - slot_ref.shape[0]

    def rows(ref, index):
        return ref.at[pl.ds(pl.multiple_of(index * nch, nch), nch), :]

    @pl.when(i == 0)
    def _():
        fill_code = pltpu.make_async_copy(pad_code_hbm, code_ref, csem)
        fill_code.start()
        fill_code.wait()
        zbuf[...] = jnp.zeros_like(zbuf)

        def region(e, carry):
            first = ps_ref[e]
            n_groups = lax.shift_right_logical(pn_ref[e], ROW_GROUP.bit_length() - 1)

            def fill_group(j, c):
                start = pl.multiple_of((first + j * ROW_GROUP) * nch, nch)
                pltpu.make_async_copy(zbuf, xs_hbm.at[pl.ds(start, ROW_GROUP * nch), :],
                                      zsem).start()
                return c

            def fill_one(j, c):
                pltpu.make_async_copy(zbuf.at[pl.ds(0, nch), :], rows(xs_hbm, first + j),
                                      zsem).start()
                return c

            carry = lax.fori_loop(0, n_groups, fill_group, carry)
            return lax.fori_loop(n_groups * ROW_GROUP, pn_ref[e], fill_one, carry)

        lax.fori_loop(0, ps_ref.shape[0], region, 0)
        pltpu.make_async_copy(xs_hbm.at[pl.ds(0, n_pad * nch), :],
                              xs_hbm.at[pl.ds(0, n_pad * nch), :], zsem).wait()

    def group(j, carry):
        for u in range(ROW_GROUP):
            r = j * ROW_GROUP + u
            for k in range(TOP_K):
                a = k * n_tok + i * tm + r
                dst = slot_ref[a]
                code_ref[dst] = a
                pltpu.make_async_copy(rows(h_ref, r), rows(xs_hbm, dst), sem).start(priority=k)
        return carry

    lax.fori_loop(0, tm // ROW_GROUP, group, 0)
    pltpu.make_async_copy(xs_hbm.at[pl.ds(0, TOP_K * tm * nch), :],
                          xs_hbm.at[pl.ds(0, TOP_K * tm * nch), :], sem).wait()

    @pl.when(i == pl.num_programs(0) - 1)
    def _():
        emit_code = pltpu.make_async_copy(code_ref, code_hbm, csem)
        emit_code.start()
        emit_code.wait()


def _send_rows(h2, slot, pad_start, pad_len, pad_code, *, tm):
    cap = pad_code.shape[0]
    nch = h2.shape[0] * TOP_K // slot.shape[0]
    n = h2.shape[0] // nch
    assert tm % ROW_GROUP == 0 and cap - slot.shape[0] > 0
    return pl.pallas_call(
        _send_rows_body,
        out_shape=[jax.ShapeDtypeStruct((cap * nch, LANES), jnp.uint32),
                   jax.ShapeDtypeStruct((cap,), jnp.int32)],
        grid_spec=pltpu.PrefetchScalarGridSpec(
            num_scalar_prefetch=3, grid=(n // tm,),
            in_specs=[pl.BlockSpec((tm * nch, LANES), lambda i, *_: (i, 0)),
                      pl.BlockSpec(memory_space=pl.ANY)],
            out_specs=[pl.BlockSpec(memory_space=pl.ANY), pl.BlockSpec(memory_space=pl.ANY)],
            scratch_shapes=[pltpu.SMEM((cap,), jnp.int32),
                            pltpu.VMEM((ROW_GROUP * nch, LANES), jnp.uint32),
                            pltpu.SemaphoreType.DMA(()), pltpu.SemaphoreType.DMA(()),
                            pltpu.SemaphoreType.DMA(())]),
        compiler_params=_params("arbitrary"),
        name="moe_send_rows",
    )(slot, pad_start, pad_len, h2, pad_code)


def _experts_body(code_ref, be_ref, nu_ref, x_ref, wg_ref, wu_ref, wd_ref,
                  out_hbm, ybuf, wg_b, wu_b, wd_b, ssem):
    nch = wg_b.shape[0] // 2 // LANES
    block_rows = ybuf.shape[0] // 2
    te = block_rows // nch
    spare = out_hbm.shape[0] - 2 * block_rows
    i = pl.program_id(0)
    n_used = nu_ref[0]

    def wait_scatter(buf):
        pltpu.make_async_copy(ybuf.at[pl.ds(0, block_rows), :],
                              out_hbm.at[pl.ds(0, block_rows), :], ssem.at[buf]).wait()

    @pl.when(i == 0)
    def _():
        ybuf[...] = jnp.zeros_like(ybuf)
        clear = pltpu.make_async_copy(ybuf, out_hbm.at[pl.ds(spare, 2 * block_rows), :], ssem.at[0])
        clear.start()
        clear.wait()

    @pl.when(i < n_used)
    def _():
        base = i * te
        buf = i % 2

        @pl.when(jnp.logical_or(i == 0, be_ref[i] != be_ref[jnp.maximum(i - 1, 0)]))
        def _():
            wg_b[...] = wg_ref[...].astype(BF16)
            wu_b[...] = wu_ref[...].astype(BF16)
            wd_b[...] = wd_ref[...].astype(BF16)

        half = nch * LANES
        x_lo, x_hi = _unpack_halves(_load_token_major(x_ref, 0, te, nch))
        x_lo, x_hi = x_lo.astype(BF16), x_hi.astype(BF16)
        g = (jnp.dot(x_lo, wg_b[:half], preferred_element_type=F32)
             + jnp.dot(x_hi, wg_b[half:], preferred_element_type=F32))
        u = (jnp.dot(x_lo, wu_b[:half], preferred_element_type=F32)
             + jnp.dot(x_hi, wu_b[half:], preferred_element_type=F32))
        act = (g * jax.nn.sigmoid(g) * u).astype(BF16)

        @pl.when(i > 1)
        def _():
            wait_scatter(buf)

        ybase = pl.multiple_of(buf * block_rows, block_rows)
        _store_token_major(ybuf, ybase, _pack_halves(jnp.dot(act, wd_b[...],
                                                              preferred_element_type=F32)))
        for r in range(te):
            src = pl.multiple_of(ybase + r * nch, nch)
            dst = pl.multiple_of(code_ref[base + r] * nch, nch)
            pltpu.make_async_copy(ybuf.at[pl.ds(src, nch), :], out_hbm.at[pl.ds(dst, nch), :],
                                  ssem.at[buf]).start(priority=r % 2)

        @pl.when(i == n_used - 1)
        def _():
            wait_scatter(buf)

            @pl.when(i > 0)
            def _():
                wait_scatter(1 - buf)


def _experts(xs, code, block_e, n_used, w_gate, w_up, w_down, *, n_assign, layer, te):
    d, de = w_gate.shape[2:]
    nch = d // 2 // LANES
    assert nch % 8 == 0, "a token must cover whole (8, 128) tiles"
    n_blocks = block_e.shape[0]
    assert code.shape[0] == n_blocks * te
    by_expert = lambda i, code, be, nu: (layer, be[i], 0, 0)
    grid_spec = pltpu.PrefetchScalarGridSpec(
        num_scalar_prefetch=3,
        grid=(n_blocks,),
        in_specs=[pl.BlockSpec((te * nch, LANES),
                               lambda i, code, be, nu: (jnp.minimum(i, nu[0] - 1), 0)),
                  pl.BlockSpec((None, None, d, de), by_expert),
                  pl.BlockSpec((None, None, d, de), by_expert),
                  pl.BlockSpec((None, None, de, d), by_expert)],
        out_specs=pl.BlockSpec(memory_space=pl.ANY),
        scratch_shapes=[pltpu.VMEM((2 * te * nch, LANES), jnp.uint32),
                        pltpu.VMEM((d, de), BF16), pltpu.VMEM((d, de), BF16),
                        pltpu.VMEM((de, d), BF16),
                        pltpu.SemaphoreType.DMA((2,))],
    )
    return pl.pallas_call(
        _experts_body,
        out_shape=jax.ShapeDtypeStruct(((n_assign + 2 * te) * nch, LANES), jnp.uint32),
        grid_spec=grid_spec,
        compiler_params=_params("arbitrary"),
        name="moe_experts",
    )(code, block_e, n_used, xs, w_gate, w_up, w_down)


def _dispatch(ri, counts, *, te):
    n = ri.shape[1]
    counts = counts[0, :N_EXPERTS]
    padded = (counts + te - 1) // te * te
    pend = jnp.cumsum(padded)
    pstart = pend - padded
    n_blocks = TOP_K * n // te + N_EXPERTS
    cap = n_blocks * te
    expert, rank = ri[:TOP_K], ri[TOP_K:2 * TOP_K]
    first = jnp.sum(jnp.where(expert[None] == jnp.arange(N_EXPERTS)[:, None, None],
                              pstart[:, None, None], 0), axis=0)
    slot = (first + rank).reshape(-1).astype(jnp.int32)
    block_start = jnp.arange(n_blocks, dtype=jnp.int32) * te
    block_e = jnp.minimum(jnp.sum(pend[None, :] <= block_start[:, None], axis=1),
                          N_EXPERTS - 1).astype(jnp.int32)
    n_used = (pend[-1:] // te).astype(jnp.int32)
    pad_start = jnp.concatenate([pstart + counts, pend[-1:]]).astype(jnp.int32)
    pad_len = jnp.concatenate([padded - counts, cap - pend[-1:]]).astype(jnp.int32)
    pad_code = TOP_K * n + jnp.arange(cap, dtype=jnp.int32) % (2 * te)
    return slot, block_e, n_used, pad_start, pad_len, pad_code


def _combine_body(x_ref, m0_ref, m1_ref, rw_ref, g2_ref, o_ref):
    o_ref[...] = _moe_residual(x_ref[...], m0_ref, m1_ref, rw_ref, g2_ref)


def _combine(x, m, rw, gate2, *, seq, tm):
    n, d = x.shape
    nch = d // 2 // LANES
    per_b = seq // tm
    row_spec = pl.BlockSpec((tm, d), lambda i: (i, 0))
    return pl.pallas_call(
        _combine_body,
        out_shape=jax.ShapeDtypeStruct((n, d), F32),
        grid=(n // tm,),
        in_specs=[row_spec,
                  pl.BlockSpec((tm * nch, LANES), lambda i: (i, 0)),
                  pl.BlockSpec((tm * nch, LANES), lambda i: (n // tm + i, 0)),
                  pl.BlockSpec((tm, rw.shape[1]), lambda i: (i, 0)),
                  pl.BlockSpec((None, 1, d), lambda i: (i // per_b, 0, 0))],
        out_specs=row_spec,
        compiler_params=_params("arbitrary"),
        name="moe_combine",
    )(x, m, m, rw, gate2)


def _tiles(seq):
    tm = min(512, seq)
    tq = min(256, seq)
    te = 512
    return tm, tq, te


def kernel(x, c, w_mod, b_mod, mod_layer, norm1_g, w_in, gm_norm_g, gm_ws, gm_bs, q_norm_g, k_norm_g, out_norm_g, w_out, norm2_g, w_group, b_group, w_route, b_route, w_gate, w_up, w_down):
    batch, seq, d = x.shape
    depth = mod_layer.shape[0]
    n = batch * seq
    tm, tq, te = _tiles(seq)
    assert seq % tm == 0 and seq % tq == 0 and tm % LANES == 0 and tq % LANES == 0
    assert (TOP_K * n) % te == 0 and w_group.shape[2] == N_GROUPS and w_route.shape[2] == N_EXPERTS
    assert w_in.shape[2] * 2 == 5 * w_out.shape[1]

    mod_all = _modulation(c, w_mod, b_mod, mod_layer).reshape(depth, batch, N_MOD, 1, d)
    w_in_b, w_out_b = w_in.astype(BF16), w_out.astype(BF16)
    pad = LANES - N_GROUPS - N_EXPERTS
    w_router = jnp.pad(jnp.concatenate([w_group, w_route], axis=2), ((0, 0), (0, 0), (0, pad)))
    w_router_hi = w_router.astype(BF16)
    w_router_lo = (w_router - w_router_hi.astype(F32)).astype(BF16)
    w_router = jnp.concatenate([w_router_hi, w_router_lo], axis=2)
    b_router = jnp.pad(jnp.concatenate([b_group, b_route], axis=1), ((0, 0), (0, pad)))

    xf = x.reshape(n, d)
    moe = None
    for l in range(depth):
        xf, ya, q, k, v = _mix_in(xf, moe, mod_all[l], norm1_g[l], w_in_b[l], gm_norm_g[l], gm_ws[l],
                                  gm_bs[l], q_norm_g[l], k_norm_g[l], out_norm_g[l], seq=seq, tm=tm)
        yb = _stick_break(q, k, v, out_norm_g[l], batch=batch, seq=seq, tq=tq)
        xf, h2, ri, rw, counts = _mix_out(xf, ya, yb, mod_all[l], w_out_b[l], norm2_g[l],
                                          w_router[l], b_router[l:l + 1], seq=seq, tm=tm)
        slot, block_e, n_used, pad_start, pad_len, pad_code = _dispatch(ri, counts, te=te)
        xs, code = _send_rows(h2, slot, pad_start, pad_len, pad_code, tm=min(4 * tm, seq))
        m = _experts(xs, code, block_e, n_used, w_gate, w_up, w_down,
                     n_assign=slot.shape[0], layer=l, te=te)
        moe = (m, rw, mod_all[l][:, 5])
    out = _combine(xf, *moe, seq=seq, tm=tm)
    return out.reshape(batch, seq, d)
```
